```python
import jax, jax.numpy as jnp
from jax import lax
import numpy as np

D_MODEL = 1024
BATCH = 8
SEQ = 2048
DEPTH = 4
DEC_BATCH = 128
DEC_SEQ = 8
PAST_LEN = 2048
PAGE_SIZE = 128

N_HEADS = 16
HEAD_DIM = D_MODEL // N_HEADS
N_A_LAYERS = DEPTH // 2
N_B_LAYERS = DEPTH - N_A_LAYERS
CONV_A_WIDTH = 31
FFN_CONV_WIDTH = 3
D_FF = ((8 * D_MODEL // 3 + 255) // 256) * 256
Q_BLOCK = 128
EPS = 1e-6
SB_BIAS_INIT = -7.0

kernel_name = 'yoco_conformer_stickbreak_decoder_step'


def rmsnorm(x, g):
    xf = x.astype(jnp.float32)
    y = xf * lax.rsqrt(jnp.mean(xf * xf, axis=-1, keepdims=True) + EPS)
    return (y * g.astype(jnp.float32)).astype(x.dtype)


def layernorm(x, g, b):
    xf = x.astype(jnp.float32)
    mu = jnp.mean(xf, axis=-1, keepdims=True)
    var = jnp.mean(jnp.square(xf - mu), axis=-1, keepdims=True)
    y = (xf - mu) * lax.rsqrt(var + EPS)
    return (y * g.astype(jnp.float32) + b.astype(jnp.float32)).astype(x.dtype)


def modulated_rmsnorm(x, g, shift, scale):
    return rmsnorm(x, g) * (1 + scale[:, None, :]) + shift[:, None, :]


def causal_depthwise_conv(x, buf, w, b):
    width, ch = w.shape
    xp = jnp.concatenate([buf.astype(x.dtype), x], axis=1)
    y = lax.conv_general_dilated(xp, w[:, None, :].astype(x.dtype), window_strides=(1,), padding='VALID',
                                 dimension_numbers=('NWC', 'WIO', 'NWC'), feature_group_count=ch)
    return y + b, xp[:, -(width - 1):, :]


def conformer_conv(h, buf, w_in, w_dw, b_dw, ln_g, ln_b, w_out):
    a, gate = jnp.split(h @ w_in, 2, axis=-1)
    u = a * jax.nn.sigmoid(gate)
    u, tail = causal_depthwise_conv(u, buf, w_dw, b_dw)
    u = layernorm(u, ln_g, ln_b)
    return jax.nn.silu(u) @ w_out, tail


def conv_ffn(h, buf, w_up, w_dw, b_dw, w_down):
    g, val = jnp.split(h @ w_up, 2, axis=-1)
    g, tail = causal_depthwise_conv(g, buf, w_dw, b_dw)
    return (jax.nn.gelu(g) * val) @ w_down, tail


def _sb_block(q, k, v, bias, q_pos, k_pos):
    z = jnp.einsum('bqhd,bkhd->bhqk', q, k).astype(jnp.float32) * (HEAD_DIM ** -0.5)
    z = z + bias.astype(jnp.float32)[None, :, None, None]
    causal = k_pos[None, :] < q_pos[:, None]
    log_1m = jnp.where(causal, jax.nn.log_sigmoid(-z), 0.0)
    suffix = lax.cumsum(log_1m, axis=3, reverse=True) - log_1m
    att = jnp.where(causal, jnp.exp(jax.nn.log_sigmoid(z) + suffix), 0.0)
    return jnp.einsum('bhqk,bkhd->bqhd', att.astype(v.dtype), v)


def stick_breaking_attention(q, k, v, bias, q_offset):
    t_q = q.shape[1]
    outs = []
    for start in range(0, t_q, Q_BLOCK):
        end = min(start + Q_BLOCK, t_q)
        k_end = q_offset + end
        q_pos = q_offset + jnp.arange(start, end)
        k_pos = jnp.arange(k_end)
        outs.append(_sb_block(q[:, start:end], k[:, :k_end], v[:, :k_end], bias, q_pos, k_pos))
    return jnp.concatenate(outs, axis=1)


def trunk(x, c, conv_a_buf, ffn_buf, past_k, past_v, q_offset,
          w_ada, b_ada, g_pre_mix, g_post_mix, g_pre_ffn, g_post_ffn,
          w_a_in, w_a_dw, b_a_dw, ln_a_g, ln_a_b, w_a_out,
          g_kv, w_kv, w_q, w_o, b_sb, w_ffn_up, w_ffn_dw, b_ffn_dw, w_ffn_down):
    bsz, t_new, _ = x.shape
    mod = jnp.einsum('bd,lde->lbe', jax.nn.silu(c), w_ada) + b_ada[:, None, :]
    conv_tails, ffn_tails = [], []
    k_all = v_all = k_new = v_new = None
    for l in range(DEPTH):
        shift_m, scale_m, gate_m, shift_f, scale_f, gate_f = jnp.split(mod[l], 6, axis=-1)
        h = modulated_rmsnorm(x, g_pre_mix[l], shift_m, scale_m)
        if l < N_A_LAYERS:
            y, tail = conformer_conv(h, conv_a_buf[l], w_a_in[l], w_a_dw[l], b_a_dw[l],
                                     ln_a_g[l], ln_a_b[l], w_a_out[l])
            conv_tails.append(tail)
        else:
            j = l - N_A_LAYERS
            q = (h @ w_q[j]).reshape(bsz, t_new, N_HEADS, HEAD_DIM)
            o = stick_breaking_attention(q, k_all, v_all, b_sb[j], q_offset)
            y = o.reshape(bsz, t_new, N_HEADS * HEAD_DIM) @ w_o[j]
        x = x + gate_m[:, None, :] * rmsnorm(y, g_post_mix[l])
        h = modulated_rmsnorm(x, g_pre_ffn[l], shift_f, scale_f)
        y, tail = conv_ffn(h, ffn_buf[l], w_ffn_up[l], w_ffn_dw[l], b_ffn_dw[l], w_ffn_down[l])
        ffn_tails.append(tail)
        x = x + gate_f[:, None, :] * rmsnorm(y, g_post_ffn[l])
        if l == N_A_LAYERS - 1:
            k_flat, v_flat = jnp.split(rmsnorm(x, g_kv) @ w_kv, 2, axis=-1)
            k_new = k_flat.reshape(bsz, t_new, N_HEADS, HEAD_DIM)
            v_new = v_flat.reshape(bsz, t_new, N_HEADS, HEAD_DIM)
            if past_k is None:
                k_all, v_all = k_new, v_new
            else:
                k_all = jnp.concatenate([past_k.astype(k_new.dtype), k_new], axis=1)
                v_all = jnp.concatenate([past_v.astype(v_new.dtype), v_new], axis=1)
    return x, jnp.stack(conv_tails), jnp.stack(ffn_tails), k_new, v_new


def setup_inputs(seed: int = 0) -> dict:
    key = jax.random.key(seed)
    ks = iter(jax.random.split(key, 40))

    def nrm(shape, scale):
        return jax.random.normal(next(ks), shape, jnp.float32) * scale

    def gain(shape):
        return 1.0 + nrm(shape, 0.05)

    n_pages = PAST_LEN // PAGE_SIZE
    n_used = DEC_BATCH * n_pages
    n_phys = n_used + n_used // 4
    page_table = jax.random.permutation(next(ks), n_phys)[:n_used].reshape(DEC_BATCH, n_pages).astype(jnp.int32)
    hd = N_HEADS * HEAD_DIM
    return {
        'x_prompt': nrm((BATCH, SEQ, D_MODEL), 1.0),
        'x_sample': nrm((DEC_BATCH, DEC_SEQ, D_MODEL), 1.0),
        'c_prompt': nrm((BATCH, D_MODEL), 1.0),
        'c_sample': nrm((DEC_BATCH, D_MODEL), 1.0),
        'state_conv_a': nrm((N_A_LAYERS, DEC_BATCH, CONV_A_WIDTH - 1, D_MODEL), 0.5),
        'state_ffn_conv': nrm((DEPTH, DEC_BATCH, FFN_CONV_WIDTH - 1, D_FF), 1.0),
        'cache_k': nrm((n_phys, PAGE_SIZE, N_HEADS, HEAD_DIM), 1.0),
        'cache_v': nrm((n_phys, PAGE_SIZE, N_HEADS, HEAD_DIM), 1.0),
        'page_table': page_table,
        'w_ada': nrm((DEPTH, D_MODEL, 6 * D_MODEL), D_MODEL ** -0.5),
        'b_ada': nrm((DEPTH, 6 * D_MODEL), 0.02),
        'g_pre_mix': gain((DEPTH, D_MODEL)),
        'g_post_mix': gain((DEPTH, D_MODEL)),
        'g_pre_ffn': gain((DEPTH, D_MODEL)),
        'g_post_ffn': gain((DEPTH, D_MODEL)),
        'w_a_in': nrm((N_A_LAYERS, D_MODEL, 2 * D_MODEL), D_MODEL ** -0.5),
        'w_a_dw': nrm((N_A_LAYERS, CONV_A_WIDTH, D_MODEL), CONV_A_WIDTH ** -0.5),
        'b_a_dw': nrm((N_A_LAYERS, D_MODEL), 0.02),
        'ln_a_g': gain((N_A_LAYERS, D_MODEL)),
        'ln_a_b': nrm((N_A_LAYERS, D_MODEL), 0.02),
        'w_a_out': nrm((N_A_LAYERS, D_MODEL, D_MODEL), D_MODEL ** -0.5),
        'g_kv': gain((D_MODEL,)),
        'w_kv': nrm((D_MODEL, 2 * hd), D_MODEL ** -0.5),
        'w_q': nrm((N_B_LAYERS, D_MODEL, hd), 0.5 * D_MODEL ** -0.5),
        'w_o': nrm((N_B_LAYERS, hd, D_MODEL), hd ** -0.5),
        'b_sb': SB_BIAS_INIT + nrm((N_B_LAYERS, N_HEADS), 0.1),
        'w_ffn_up': nrm((DEPTH, D_MODEL, 2 * D_FF), D_MODEL ** -0.5),
        'w_ffn_dw': nrm((DEPTH, FFN_CONV_WIDTH, D_FF), FFN_CONV_WIDTH ** -0.5),
        'b_ffn_dw': nrm((DEPTH, D_FF), 0.02),
        'w_ffn_down': nrm((DEPTH, D_FF, D_MODEL), D_FF ** -0.5),
    }


def reference(x_prompt, x_sample, c_prompt, c_sample, state_conv_a, state_ffn_conv, cache_k, cache_v, page_table,
              w_ada, b_ada, g_pre_mix, g_post_mix, g_pre_ffn, g_post_ffn,
              w_a_in, w_a_dw, b_a_dw, ln_a_g, ln_a_b, w_a_out,
              g_kv, w_kv, w_q, w_o, b_sb, w_ffn_up, w_ffn_dw, b_ffn_dw, w_ffn_down):
    n_prompt = x_prompt.shape[0]
    conv_a_zero = jnp.zeros((N_A_LAYERS, n_prompt, CONV_A_WIDTH - 1, D_MODEL), x_prompt.dtype)
    ffn_zero = jnp.zeros((DEPTH, n_prompt, FFN_CONV_WIDTH - 1, D_FF), x_prompt.dtype)
    y_prompt, conv_a_prompt, ffn_conv_prompt, k_prompt, v_prompt = trunk(
        x_prompt, c_prompt, conv_a_zero, ffn_zero, None, None, 0,
        w_ada, b_ada, g_pre_mix, g_post_mix, g_pre_ffn, g_post_ffn,
        w_a_in, w_a_dw, b_a_dw, ln_a_g, ln_a_b, w_a_out,
        g_kv, w_kv, w_q, w_o, b_sb, w_ffn_up, w_ffn_dw, b_ffn_dw, w_ffn_down)
    n_seq, n_pages = page_table.shape
    past_len = n_pages * PAGE_SIZE
    past_k = cache_k[page_table].reshape(n_seq, past_len, N_HEADS, HEAD_DIM)
    past_v = cache_v[page_table].reshape(n_seq, past_len, N_HEADS, HEAD_DIM)
    y_sample, conv_a_sample, ffn_conv_sample, k_sample, v_sample = trunk(
        x_sample, c_sample, state_conv_a, state_ffn_conv, past_k, past_v, past_len,
        w_ada, b_ada, g_pre_mix, g_post_mix, g_pre_ffn, g_post_ffn,
        w_a_in, w_a_dw, b_a_dw, ln_a_g, ln_a_b, w_a_out,
        g_kv, w_kv, w_q, w_o, b_sb, w_ffn_up, w_ffn_dw, b_ffn_dw, w_ffn_down)
    return (y_prompt, y_sample, conv_a_prompt, conv_a_sample, ffn_conv_prompt, ffn_conv_sample,
            k_prompt, v_prompt, k_sample, v_sample)
```

```python
import functools

import jax
import jax.numpy as jnp
from jax import lax
from jax.experimental import pallas as pl
from jax.experimental.pallas import tpu as pltpu

F32 = jnp.float32
BF16 = jnp.bfloat16
EPS = 1e-6
LANES = 128
SUBLANES = 8
CONV_PAD = 32
VMEM_LIMIT = 56 * 1024 * 1024
GELU_C = 0.7978845608028654


def _cparams(*sem):
    return pltpu.CompilerParams(dimension_semantics=sem, vmem_limit_bytes=VMEM_LIMIT)


def _const_spec(shape, layer=None):
    nd = len(shape)
    if layer is None:
        return pl.BlockSpec(shape, lambda *_: (0,) * nd, pipeline_mode=pl.Buffered(1))
    return pl.BlockSpec((None,) + tuple(shape), lambda *_: (layer,) + (0,) * nd,
                        pipeline_mode=pl.Buffered(1))


def _pick_tile(n, target):
    best = None
    for t in range(LANES, min(n, target) + 1, LANES):
        if n % t == 0:
            best = t
    assert best is not None
    return best


def _rms(x, g):
    ms = jnp.mean(x * x, axis=-1, keepdims=True)
    return x * lax.rsqrt(ms + EPS) * g


def _dot(a, b):
    return jnp.dot(a, b, preferred_element_type=F32)


def _dot_nt(a, b):
    return lax.dot_general(a, b, (((1,), (1,)), ((), ())), preferred_element_type=F32)


def _neg_softplus(z):
    return -(jnp.maximum(z, 0.0) + jnp.log(1.0 + jnp.exp(-jnp.abs(z))))


def _idiv(x, n):
    assert n & (n - 1) == 0, "power-of-two divisor"
    return lax.shift_right_logical(x, n.bit_length() - 1)


def _imod(x, n):
    assert n & (n - 1) == 0, "power-of-two divisor"
    return x & (n - 1)


def _rev_cumsum(l, tri):
    hi = l.astype(BF16)
    lo = (l - hi.astype(F32)).astype(BF16)
    return _dot(hi, tri) + _dot(lo, tri)


def _ada_body(cp_ref, cs_ref, w_ref, b_ref, op_ref, os_ref):
    w = w_ref[...].astype(BF16)
    b = b_ref[...]
    for c_ref, o_ref in ((cp_ref, op_ref), (cs_ref, os_ref)):
        c = c_ref[...]
        a = (c * jax.nn.sigmoid(c)).astype(BF16)
        o_ref[...] = _dot(a, w) + b


def _ada(c_prompt, c_sample, w_ada, b_ada):
    depth, d, n = w_ada.shape
    tn = _pick_tile(n, 1024)
    bp, bs = c_prompt.shape[0], c_sample.shape[0]
    return pl.pallas_call(
        _ada_body,
        grid=(depth, n // tn),
        in_specs=[
            pl.BlockSpec((bp, d), lambda l, j: (0, 0)),
            pl.BlockSpec((bs, d), lambda l, j: (0, 0)),
            pl.BlockSpec((None, d, tn), lambda l, j: (l, 0, j)),
            pl.BlockSpec((None, 1, tn), lambda l, j: (l, 0, j)),
        ],
        out_specs=[
            pl.BlockSpec((None, bp, tn), lambda l, j: (l, 0, j)),
            pl.BlockSpec((None, bs, tn), lambda l, j: (l, 0, j)),
        ],
        out_shape=[jax.ShapeDtypeStruct((depth, bp, n), F32),
                   jax.ShapeDtypeStruct((depth, bs, n), F32)],
        compiler_params=_cparams("arbitrary", "arbitrary"),
        name="ada",
    )(c_prompt, c_sample, w_ada, b_ada.reshape(depth, 1, n))


class _Layout:
    def __init__(self, x_shape, prompt, row_tile, seq_tile):
        ng, rt, _ = x_shape
        self.prompt = prompt
        if prompt:
            self.g, self.r = 1, min(row_tile, rt)
            assert rt % self.r == 0
            self.grid = (ng, rt // self.r)
        else:
            self.g, self.r = min(seq_tile, ng), rt
            assert ng % self.g == 0
            self.grid = (ng // self.g, 1)
        self.nt = self.grid[1]

    def act(self, d):
        return pl.BlockSpec((self.g, self.r, d), lambda i, t: (i, t, 0))

    def per_group(self, rows, d, layer=None):
        if layer is None:
            return pl.BlockSpec((self.g, rows, d), lambda i, t: (i, 0, 0))
        return pl.BlockSpec((None, self.g, rows, d), lambda i, t: (layer, i, 0, 0))


def _dwconv(xs, w_ref, b_ref, out, g, r, d, taps):
    off0 = CONV_PAD - (taps - 1)
    if g == 1:
        gc, rc = 1, min(r, 32)
    else:
        gc, rc = min(g, 4), r

    def strip(j, carry):
        cs = pl.ds(pl.multiple_of(j * LANES, LANES), LANES)
        bias = b_ref[:, cs]
        for g0 in range(0, g, gc):
            for r0 in range(0, r, rc):
                acc = jnp.broadcast_to(bias, (gc, rc, LANES))
                for k in range(taps):
                    lo = r0 + off0 + k
                    acc = acc + w_ref[k:k + 1, cs] * xs[g0:g0 + gc, lo:lo + rc, cs]
                out[g0:g0 + gc, r0:r0 + rc, cs] = acc
        return carry

    lax.fori_loop(0, d // LANES, strip, 0)


def _mixer_a_body(lay, d, taps, *refs):
    if lay.prompt:
        (x_ref, mod_ref, gpre, gpost, win, wdw, bdw, lng, lnb, wout,
         o_ref, tail_ref, xs, cv) = refs
        st_ref = None
    else:
        (x_ref, mod_ref, st_ref, gpre, gpost, win, wdw, bdw, lng, lnb, wout,
         o_ref, tail_ref, xs, cv) = refs
    g, r = lay.g, lay.r
    m = g * r
    t = pl.program_id(1)
    if lay.prompt:
        @pl.when(t == 0)
        def _():
            xs[:, 0:CONV_PAD, :] = jnp.zeros((g, CONV_PAD, d), F32)
    else:
        xs[:, CONV_PAD - (taps - 1):CONV_PAD, :] = st_ref[...]

    x3 = x_ref[...]
    shift, scale, gate = mod_ref[:, 0:1, :], mod_ref[:, 1:2, :], mod_ref[:, 2:3, :]
    h3 = _rms(x3, gpre[...]) * (1.0 + scale) + shift
    ag = _dot(h3.reshape(m, d).astype(BF16), win[...])
    u = ag[:, :d] * jax.nn.sigmoid(ag[:, d:])
    xs[:, CONV_PAD:CONV_PAD + r, :] = u.reshape(g, r, d)
    _dwconv(xs, wdw, bdw, cv, g, r, d, taps)
    c3 = cv[...]
    mu = jnp.mean(c3, axis=-1, keepdims=True)
    dc = c3 - mu
    var = jnp.mean(dc * dc, axis=-1, keepdims=True)
    yn = dc * lax.rsqrt(var + EPS) * lng[...] + lnb[...]
    s = yn * jax.nn.sigmoid(yn)
    y = _dot(s.reshape(m, d).astype(BF16), wout[...])
    o_ref[...] = x3 + gate * _rms(y.reshape(g, r, d), gpost[...])

    lo = r + CONV_PAD - (taps - 1)
    if lay.prompt:
        @pl.when(t == lay.nt - 1)
        def _():
            tail_ref[...] = xs[:, lo:r + CONV_PAD, :]
        xs[:, 0:CONV_PAD, :] = xs[:, r:r + CONV_PAD, :]
    else:
        tail_ref[...] = xs[:, lo:r + CONV_PAD, :]


def _mixer_a(lay, l, x, mod, state, gpre, gpost, win, wdw, bdw, lng, lnb, wout):
    ng, rt, d = x.shape
    taps = wdw.shape[1]
    in_specs = [lay.act(d), lay.per_group(6, d, l)]
    args = [x, mod]
    if not lay.prompt:
        in_specs.append(lay.per_group(taps - 1, d, l))
        args.append(state)
    in_specs += [_const_spec((1, d), l), _const_spec((1, d), l), _const_spec((d, 2 * d), l),
                 _const_spec((taps, d), l), _const_spec((1, d), l), _const_spec((1, d), l),
                 _const_spec((1, d), l), _const_spec((d, d), l)]
    args += [gpre, gpost, win, wdw, bdw, lng, lnb, wout]
    return pl.pallas_call(
        functools.partial(_mixer_a_body, lay, d, taps),
        grid=lay.grid,
        in_specs=in_specs,
        out_specs=[lay.act(d), lay.per_group(taps - 1, d)],
        out_shape=[jax.ShapeDtypeStruct(x.shape, F32),
                   jax.ShapeDtypeStruct((ng, taps - 1, d), F32)],
        scratch_shapes=[pltpu.VMEM((lay.g, lay.r + CONV_PAD, d), F32),
                        pltpu.VMEM((lay.g, lay.r, d), F32)],
        compiler_params=_cparams("arbitrary", "arbitrary"),
        name="mixer_a_p" if lay.prompt else "mixer_a_s",
    )(*args)


def _ffn_body(lay, d, dff, fc, *refs):
    if lay.prompt:
        (x_ref, mod_ref, gpre, gpost, wup, wdw, bdw, wdown,
         o_ref, tail_ref, gbuf, acc) = refs
        st_ref = None
    else:
        (x_ref, mod_ref, st_ref, gpre, gpost, wup, wdw, bdw, wdown,
         o_ref, tail_ref, gbuf, acc) = refs
    g, r = lay.g, lay.r
    m = g * r
    t = pl.program_id(1)
    if lay.prompt:
        @pl.when(t == 0)
        def _():
            gbuf[:, 0:SUBLANES, :] = jnp.zeros((g, SUBLANES, dff), F32)
    else:
        gbuf[:, SUBLANES - 2:SUBLANES, :] = st_ref[...]

    x3 = x_ref[...]
    shift, scale, gate = mod_ref[:, 3:4, :], mod_ref[:, 4:5, :], mod_ref[:, 5:6, :]
    hb = (_rms(x3, gpre[...]) * (1.0 + scale) + shift).reshape(m, d).astype(BF16)
    for c in range(dff // fc):
        cs = slice(c * fc, (c + 1) * fc)
        gt = _dot(hb, wup[:, c * fc:(c + 1) * fc]).reshape(g, r, fc)
        val = _dot(hb, wup[:, dff + c * fc:dff + (c + 1) * fc]).reshape(g, r, fc)
        gbuf[:, SUBLANES:SUBLANES + r, cs] = gt
        g1 = gbuf[:, SUBLANES - 1:SUBLANES - 1 + r, cs]
        g2 = gbuf[:, SUBLANES - 2:SUBLANES - 2 + r, cs]
        y = wdw[0:1, cs] * g2 + wdw[1:2, cs] * g1 + wdw[2:3, cs] * gt + bdw[:, cs]
        act = 0.5 * y * (1.0 + jnp.tanh(GELU_C * (y + 0.044715 * (y * y * y)))) * val
        contrib = _dot(act.reshape(m, fc).astype(BF16), wdown[cs, :])
        if c == 0:
            acc[...] = contrib
        else:
            acc[...] += contrib
    o_ref[...] = x3 + gate * _rms(acc[...].reshape(g, r, d), gpost[...])

    if lay.prompt:
        @pl.when(t == lay.nt - 1)
        def _():
            tail_ref[...] = gbuf[:, r + SUBLANES - 2:r + SUBLANES, :]
        gbuf[:, 0:SUBLANES, :] = gbuf[:, r:r + SUBLANES, :]
    else:
        tail_ref[...] = gbuf[:, r + SUBLANES - 2:r + SUBLANES, :]


def _ffn(lay, l, x, mod, state, gpre, gpost, wup, wdw, bdw, wdown):
    ng, rt, d = x.shape
    dff = wdown.shape[1]
    fc = 256
    assert dff % fc == 0
    in_specs = [lay.act(d), lay.per_group(6, d, l)]
    args = [x, mod]
    if not lay.prompt:
        in_specs.append(lay.per_group(2, dff, l))
        args.append(state)
    in_specs += [_const_spec((1, d), l), _const_spec((1, d), l), _const_spec((d, 2 * dff), l),
                 _const_spec((3, dff), l), _const_spec((1, dff), l), _const_spec((dff, d), l)]
    args += [gpre, gpost, wup, wdw, bdw, wdown]
    return pl.pallas_call(
        functools.partial(_ffn_body, lay, d, dff, fc),
        grid=lay.grid,
        in_specs=in_specs,
        out_specs=[lay.act(d), lay.per_group(2, dff)],
        out_shape=[jax.ShapeDtypeStruct(x.shape, F32),
                   jax.ShapeDtypeStruct((ng, 2, dff), F32)],
        scratch_shapes=[pltpu.VMEM((lay.g, lay.r + SUBLANES, dff), F32),
                        pltpu.VMEM((lay.g * lay.r, d), F32)],
        compiler_params=_cparams("arbitrary", "arbitrary"),
        name="ffn_p" if lay.prompt else "ffn_s",
    )(*args)


def _kv_body(lay, d, hd, x_ref, g_ref, w_ref, k_ref, v_ref, *bf16_refs):
    m = lay.g * lay.r
    h = _rms(x_ref[...], g_ref[...]).reshape(m, d).astype(BF16)
    kv = _dot(h, w_ref[...])
    k = kv[:, :hd].reshape(lay.g, lay.r, hd)
    v = kv[:, hd:].reshape(lay.g, lay.r, hd)
    k_ref[...] = k
    v_ref[...] = v
    if bf16_refs:
        bf16_refs[0][...] = k.astype(BF16)
        bf16_refs[1][...] = v.astype(BF16)


def _kv_proj(lay, x, g_kv, w_kv):
    ng, rt, d = x.shape
    hd = w_kv.shape[1] // 2
    sd = jax.ShapeDtypeStruct
    out_shape = [sd((ng, rt, hd), F32), sd((ng, rt, hd), F32)]
    if lay.prompt:
        out_shape += [sd((ng, rt, hd), BF16), sd((ng, rt, hd), BF16)]
    return pl.pallas_call(
        functools.partial(_kv_body, lay, d, hd),
        grid=lay.grid,
        in_specs=[lay.act(d), _const_spec((1, d)), _const_spec((d, 2 * hd))],
        out_specs=[lay.act(hd)] * len(out_shape),
        out_shape=out_shape,
        compiler_params=_cparams("arbitrary", "arbitrary"),
        name="kv_p" if lay.prompt else "kv_s",
    )(x, g_kv, w_kv)


def _q_body(lay, d, hd, qscale, x_ref, mod_ref, gpre, w_ref, q_ref):
    m = lay.g * lay.r
    shift, scale = mod_ref[:, 0:1, :], mod_ref[:, 1:2, :]
    h = (_rms(x_ref[...], gpre[...]) * (1.0 + scale) + shift).reshape(m, d).astype(BF16)
    q = _dot(h, w_ref[...]) * qscale
    q_ref[...] = q.reshape(lay.g, lay.r, hd).astype(q_ref.dtype)


def _q_proj(lay, l, j, x, mod, gpre, w_q, qscale):
    ng, rt, d = x.shape
    hd = w_q.shape[2]
    return pl.pallas_call(
        functools.partial(_q_body, lay, d, hd, qscale),
        grid=lay.grid,
        in_specs=[lay.act(d), lay.per_group(6, d, l), _const_spec((1, d), l),
                  _const_spec((d, hd), j)],
        out_specs=lay.act(hd),
        out_shape=jax.ShapeDtypeStruct((ng, rt, hd), BF16 if lay.prompt else F32),
        compiler_params=_cparams("arbitrary", "arbitrary"),
        name="q_p" if lay.prompt else "q_s",
    )(x, mod, gpre, w_q)


def _o_body(lay, d, hd, o_ref, x_ref, mod_ref, gpost, w_ref, out_ref):
    m = lay.g * lay.r
    gate = mod_ref[:, 2:3, :]
    y = _dot(o_ref[...].reshape(m, hd).astype(BF16), w_ref[...])
    out_ref[...] = x_ref[...] + gate * _rms(y.reshape(lay.g, lay.r, d), gpost[...])


def _o_proj(lay, l, j, o, x, mod, gpost, w_o):
    ng, rt, d = x.shape
    hd = w_o.shape[1]
    return pl.pallas_call(
        functools.partial(_o_body, lay, d, hd),
        grid=lay.grid,
        in_specs=[lay.act(hd), lay.act(d), lay.per_group(6, d, l), _const_spec((1, d), l),
                  _const_spec((hd, d), j)],
        out_specs=lay.act(d),
        out_shape=jax.ShapeDtypeStruct(x.shape, F32),
        compiler_params=_cparams("arbitrary", "arbitrary"),
        name="o_p" if lay.prompt else "o_s",
    )(o, x, mod, gpost, w_o)


def _attn_p_body(blk, dh, bias_ref, q_ref, k_ref, v_ref, tri_ref, o_ref, acc):
    hp = pl.program_id(1)
    i = pl.program_id(2)
    lane = lax.broadcasted_iota(jnp.int32, (blk, LANES), 1)
    q = q_ref[...]
    zero = jnp.zeros_like(q)
    qm = (jnp.where(lane < dh, q, zero), jnp.where(lane >= dh, q, zero))
    bias = (bias_ref[2 * hp], bias_ref[2 * hp + 1])
    tri = tri_ref[...]
    row = lax.broadcasted_iota(jnp.int32, (blk, blk), 0)
    col = lax.broadcasted_iota(jnp.int32, (blk, blk), 1)
    causal = col < row

    kb = k_ref[pl.ds(pl.multiple_of(i * blk, blk), blk), :]
    vb = v_ref[pl.ds(pl.multiple_of(i * blk, blk), blk), :]
    carry0 = []
    for hh in range(2):
        z = _dot_nt(qm[hh], kb) + bias[hh]
        l = jnp.where(causal, _neg_softplus(z), 0.0)
        c = _rev_cumsum(l, tri)
        att = jnp.where(causal, jnp.exp(z + c), 0.0)
        acc[hh] = _dot(att.astype(BF16), vb)
        carry0.append(c[:, 0:1])

    def step(s, carry):
        j = i - 1 - s
        kb = k_ref[pl.ds(pl.multiple_of(j * blk, blk), blk), :]
        vb = v_ref[pl.ds(pl.multiple_of(j * blk, blk), blk), :]
        new = []
        for hh in range(2):
            z = _dot_nt(qm[hh], kb) + bias[hh]
            l = _neg_softplus(z)
            c = _rev_cumsum(l, tri)
            att = jnp.exp(z + carry[hh] + c)
            acc[hh] += _dot(att.astype(BF16), vb)
            new.append(carry[hh] + c[:, 0:1])
        return tuple(new)

    lax.fori_loop(0, i, step, tuple(carry0))
    o_ref[...] = jnp.where(lane < dh, acc[0], acc[1]).astype(o_ref.dtype)


def _attn_prompt(q, kb, vb, bias, n_heads):
    b, t, hd = q.shape
    dh = hd // n_heads
    assert 2 * dh == LANES and n_heads % 2 == 0
    blk = min(256, t)
    assert t % blk == 0
    tri = (jnp.arange(blk)[:, None] >= jnp.arange(blk)[None, :]).astype(BF16)
    return pl.pallas_call(
        functools.partial(_attn_p_body, blk, dh),
        grid=(b, n_heads // 2, t // blk),
        in_specs=[
            pl.BlockSpec(memory_space=pltpu.SMEM),
            pl.BlockSpec((None, blk, LANES), lambda bi, hp, i: (bi, i, hp)),
            pl.BlockSpec((None, t, LANES), lambda bi, hp, i: (bi, 0, hp)),
            pl.BlockSpec((None, t, LANES), lambda bi, hp, i: (bi, 0, hp)),
            pl.BlockSpec((blk, blk), lambda bi, hp, i: (0, 0)),
        ],
        out_specs=pl.BlockSpec((None, blk, LANES), lambda bi, hp, i: (bi, i, hp)),
        out_shape=jax.ShapeDtypeStruct((b, t, hd), BF16),
        scratch_shapes=[pltpu.VMEM((2, blk, LANES), F32)],
        compiler_params=_cparams("arbitrary", "arbitrary", "arbitrary"),
        name="attn_p",
    )(bias, q, kb, vb, tri)


def _attn_s_body(n_pages, page, r, n_heads, dh, pt_ref, bias_ref, q_ref, kn_ref, vn_ref, tri_ref,
                 *refs):
    del pt_ref
    k_refs = refs[:n_pages]
    v_refs = refs[n_pages:2 * n_pages]
    o_ref = refs[2 * n_pages]
    acc = refs[2 * n_pages + 1]
    hd = n_heads * dh
    hq = n_heads * r
    row_h = _idiv(lax.broadcasted_iota(jnp.int32, (hq, hd), 0), r)
    lane_h = _idiv(lax.broadcasted_iota(jnp.int32, (hq, hd), 1), dh)
    own = row_h == lane_h
    q = q_ref[...]
    qrep = jnp.broadcast_to(q[None], (n_heads, r, hd)).reshape(hq, hd)
    qbd = jnp.where(own, qrep, 0.0).astype(BF16)
    rh = _idiv(lax.broadcasted_iota(jnp.int32, (hq, page), 0), r)
    bias = jnp.zeros((hq, page), F32)
    for h in range(n_heads):
        bias = jnp.where(rh == h, bias_ref[h], bias)
    tri = tri_ref[...]

    qi = _imod(lax.broadcasted_iota(jnp.int32, (hq, page), 0), r)
    kj = lax.broadcasted_iota(jnp.int32, (hq, page), 1)
    causal = kj < qi
    pad = jnp.zeros((page - r, hd), F32)
    kn = jnp.concatenate([kn_ref[...], pad], axis=0).astype(BF16)
    vn = jnp.concatenate([vn_ref[...], pad], axis=0).astype(BF16)
    z = _dot_nt(qbd, kn) + bias
    l = jnp.where(causal, _neg_softplus(z), 0.0)
    c = _rev_cumsum(l, tri)
    att = jnp.where(causal, jnp.exp(z + c), 0.0)
    acc[...] = _dot(att.astype(BF16), vn)
    carry = c[:, 0:1]
    for p in range(n_pages - 1, -1, -1):
        kb = k_refs[p][...].astype(BF16)
        vb = v_refs[p][...].astype(BF16)
        z = _dot_nt(qbd, kb) + bias
        l = _neg_softplus(z)
        c = _rev_cumsum(l, tri)
        att = jnp.exp(z + carry + c)
        acc[...] += _dot(att.astype(BF16), vb)
        carry = carry + c[:, 0:1]
    o = jnp.where(own, acc[...], 0.0).reshape(n_heads, r, hd)
    o_ref[...] = jnp.sum(o, axis=0).astype(o_ref.dtype)


def _attn_sample(q, k_new, v_new, cache_k, cache_v, page_table, bias, n_heads):
    ns, r, hd = q.shape
    dh = hd // n_heads
    n_phys, page = cache_k.shape[0], cache_k.shape[1]
    n_pages = page_table.shape[1]
    ck = cache_k.reshape(n_phys, page, hd)
    cv = cache_v.reshape(n_phys, page, hd)
    tri = (jnp.arange(page)[:, None] >= jnp.arange(page)[None, :]).astype(BF16)
    seq = pl.BlockSpec((None, r, hd), lambda s, pt: (s, 0, 0))

    def page_spec(p):
        return pl.BlockSpec((None, page, hd), lambda s, pt: (pt[s * n_pages + p], 0, 0))

    grid_spec = pltpu.PrefetchScalarGridSpec(
        num_scalar_prefetch=1,
        grid=(ns,),
        in_specs=[pl.BlockSpec(memory_space=pltpu.SMEM), seq, seq, seq,
                  pl.BlockSpec((page, page), lambda s, pt: (0, 0))]
        + [page_spec(p) for p in range(n_pages)] * 2,
        out_specs=seq,
        scratch_shapes=[pltpu.VMEM((n_heads * r, hd), F32)],
    )
    return pl.pallas_call(
        functools.partial(_attn_s_body, n_pages, page, r, n_heads, dh),
        grid_spec=grid_spec,
        out_shape=jax.ShapeDtypeStruct((ns, r, hd), F32),
        compiler_params=_cparams("arbitrary"),
        name="attn_s",
    )(page_table.reshape(-1), bias, q, k_new, v_new, tri,
      *([ck] * n_pages), *([cv] * n_pages))


def kernel(x_prompt, x_sample, c_prompt, c_sample, state_conv_a, state_ffn_conv, cache_k, cache_v, page_table, w_ada, b_ada, g_pre_mix, g_post_mix, g_pre_ffn, g_post_ffn, w_a_in, w_a_dw, b_a_dw, ln_a_g, ln_a_b, w_a_out, g_kv, w_kv, w_q, w_o, b_sb, w_ffn_up, w_ffn_dw, b_ffn_dw, w_ffn_down):
    depth, d = g_pre_mix.shape
    n_a = w_a_in.shape[0]
    n_heads, dh = cache_k.shape[2], cache_k.shape[3]
    hd = n_heads * dh
    dff = w_ffn_down.shape[1]

    def vec(a):
        return a.reshape(a.shape[0], 1, a.shape[1])

    g_pre_mix, g_post_mix, g_pre_ffn, g_post_ffn = map(vec, (g_pre_mix, g_post_mix, g_pre_ffn, g_post_ffn))
    b_a_dw, ln_a_g, ln_a_b, b_ffn_dw = map(vec, (b_a_dw, ln_a_g, ln_a_b, b_ffn_dw))
    w_a_in, w_a_out, w_kv, w_q, w_o, w_ffn_up, w_ffn_down = (
        w.astype(BF16) for w in (w_a_in, w_a_out, w_kv, w_q, w_o, w_ffn_up, w_ffn_down))
    g_kv = g_kv.reshape(1, d)

    mod_p, mod_s = _ada(c_prompt, c_sample, w_ada, b_ada)
    groups = [
        dict(x=x_prompt, mod=mod_p.reshape(depth, -1, 6, d), conv_state=None, ffn_state=None,
             lay_a=_Layout(x_prompt.shape, True, 256, 0), lay_f=_Layout(x_prompt.shape, True, 256, 0),
             lay_d=_Layout(x_prompt.shape, True, 512, 0)),
        dict(x=x_sample, mod=mod_s.reshape(depth, -1, 6, d), conv_state=state_conv_a,
             ffn_state=state_ffn_conv,
             lay_a=_Layout(x_sample.shape, False, 0, 32), lay_f=_Layout(x_sample.shape, False, 0, 32),
             lay_d=_Layout(x_sample.shape, False, 0, 64)),
    ]
    results = []
    for grp in groups:
        x, mod = grp["x"], grp["mod"]
        prompt = grp["lay_a"].prompt
        conv_tails, ffn_tails = [], []
        for l in range(depth):
            if l < n_a:
                x, tail = _mixer_a(grp["lay_a"], l, x, mod, grp["conv_state"], g_pre_mix, g_post_mix,
                                   w_a_in, w_a_dw, b_a_dw, ln_a_g, ln_a_b, w_a_out)
                conv_tails.append(tail)
            else:
                j = l - n_a
                q = _q_proj(grp["lay_d"], l, j, x, mod, g_pre_mix, w_q, dh ** -0.5)
                if prompt:
                    o = _attn_prompt(q, *kv_bf16, b_sb[j], n_heads)
                else:
                    o = _attn_sample(q, k_new, v_new, cache_k, cache_v, page_table, b_sb[j], n_heads)
                x = _o_proj(grp["lay_d"], l, j, o, x, mod, g_post_mix, w_o)
            x, tail = _ffn(grp["lay_f"], l, x, mod, grp["ffn_state"], g_pre_ffn, g_post_ffn,
                           w_ffn_up, w_ffn_dw, b_ffn_dw, w_ffn_down)
            ffn_tails.append(tail)
            if l == n_a - 1:
                k_new, v_new, *kv_bf16 = _kv_proj(grp["lay_d"], x, g_kv, w_kv)
        shp = x.shape[:2] + (n_heads, dh)
        results.append((x, jnp.stack(conv_tails), jnp.stack(ffn_tails),
                        k_new.reshape(shp), v_new.reshape(shp)))
    (yp, cap, ffp, kp, vp), (ys, cas, ffs, ks, vs) = results
    return (yp, ys, cap, cas, ffp, ffs, kp, vp, ks, vs)
```

```python
import functools

import jax
import jax.numpy as jnp
from jax import lax
from jax.experimental import pallas as pl
from jax.experimental.pallas import tpu as pltpu

F32 = jnp.float32
BF16 = jnp.bfloat16
EPS = 1e-6
LANES = 128
SUBLANES = 8
CONV_PAD = 32
VMEM_LIMIT = 56 * 1024 * 1024
GELU_C = 0.7978845608028654
LOG2E = 1.4426950408889634
SIGN_BIT = 0x80000000


def _cparams(*sem):
    return pltpu.CompilerParams(dimension_semantics=sem, vmem_limit_bytes=VMEM_LIMIT)


def _const_spec(shape, layer=None):
    nd = len(shape)
    if layer is None:
        return pl.BlockSpec(shape, lambda *_: (0,) * nd, pipeline_mode=pl.Buffered(1))
    return pl.BlockSpec((None,) + tuple(shape), lambda *_: (layer,) + (0,) * nd,
                        pipeline_mode=pl.Buffered(1))


def _pick_tile(n, target):
    best = None
    for t in range(LANES, min(n, target) + 1, LANES):
        if n % t == 0:
            best = t
    assert best is not None
    return best


def _rms(x, g):
    ms = jnp.mean(x * x, axis=-1, keepdims=True)
    return x * lax.rsqrt(ms + EPS) * g


def _dot(a, b):
    return jnp.dot(a, b, preferred_element_type=F32)


def _dot_nt(a, b):
    return lax.dot_general(a, b, (((1,), (1,)), ((), ())), preferred_element_type=F32)


def _softplus2(z):
    bits = lax.bitcast_convert_type(z, jnp.uint32) | jnp.uint32(SIGN_BIT)
    neg_abs = lax.bitcast_convert_type(bits, F32)
    return jnp.maximum(z, 0.0) + jnp.log2(1.0 + jnp.exp2(neg_abs))


def _idiv(x, n):
    assert n & (n - 1) == 0, "power-of-two divisor"
    return lax.shift_right_logical(x, n.bit_length() - 1)


def _imod(x, n):
    assert n & (n - 1) == 0, "power-of-two divisor"
    return x & (n - 1)


def _ada_body(cp_ref, cs_ref, w_ref, b_ref, op_ref, os_ref):
    w = w_ref[...].astype(BF16)
    b = b_ref[...]
    for c_ref, o_ref in ((cp_ref, op_ref), (cs_ref, os_ref)):
        c = c_ref[...]
        a = (c * jax.nn.sigmoid(c)).astype(BF16)
        o_ref[...] = _dot(a, w) + b


def _ada(c_prompt, c_sample, w_ada, b_ada):
    depth, d, n = w_ada.shape
    tn = _pick_tile(n, 1024)
    bp, bs = c_prompt.shape[0], c_sample.shape[0]
    return pl.pallas_call(
        _ada_body,
        grid=(depth, n // tn),
        in_specs=[
            pl.BlockSpec((bp, d), lambda l, j: (0, 0)),
            pl.BlockSpec((bs, d), lambda l, j: (0, 0)),
            pl.BlockSpec((None, d, tn), lambda l, j: (l, 0, j)),
            pl.BlockSpec((None, 1, tn), lambda l, j: (l, 0, j)),
        ],
        out_specs=[
            pl.BlockSpec((None, bp, tn), lambda l, j: (l, 0, j)),
            pl.BlockSpec((None, bs, tn), lambda l, j: (l, 0, j)),
        ],
        out_shape=[jax.ShapeDtypeStruct((depth, bp, n), F32),
                   jax.ShapeDtypeStruct((depth, bs, n), F32)],
        compiler_params=_cparams("arbitrary", "arbitrary"),
        name="ada",
    )(c_prompt, c_sample, w_ada, b_ada.reshape(depth, 1, n))


class _Layout:
    def __init__(self, x_shape, prompt, row_tile, seq_tile):
        ng, rt, _ = x_shape
        self.prompt = prompt
        if prompt:
            self.g, self.r = 1, min(row_tile, rt)
            assert rt % self.r == 0
            self.grid = (ng, rt // self.r)
        else:
            self.g, self.r = min(seq_tile, ng), rt
            assert ng % self.g == 0
            self.grid = (ng // self.g, 1)
        self.nt = self.grid[1]

    def act(self, d):
        return pl.BlockSpec((self.g, self.r, d), lambda i, t: (i, t, 0))

    def per_group(self, rows, d, layer=None):
        if layer is None:
            return pl.BlockSpec((self.g, rows, d), lambda i, t: (i, 0, 0))
        return pl.BlockSpec((None, self.g, rows, d), lambda i, t: (layer, i, 0, 0))


def _dwconv(xs, w_ref, b_ref, out, g, r, d, taps):
    off0 = CONV_PAD - (taps - 1)
    gc = min(g, 4)
    rc = min(r, 64)

    def strip_rows(j, carry):
        cs = pl.ds(pl.multiple_of(j * LANES, LANES), LANES)
        bias = b_ref[:, cs]
        for r0 in range(0, r, rc):
            acc = jnp.broadcast_to(bias, (rc, LANES))
            for res in range(SUBLANES):
                z = None
                for k in range(taps):
                    if (k + off0) % SUBLANES != res:
                        continue
                    base = r0 + k + off0 - res
                    term = w_ref[k:k + 1, cs] * xs[0, base:base + rc + SUBLANES, cs]
                    z = term if z is None else z + term
                if z is not None:
                    acc = acc + z[res:res + rc]
            out[0, r0:r0 + rc, cs] = acc
        return carry

    def strip_groups(j, carry):
        cs = pl.ds(pl.multiple_of(j * LANES, LANES), LANES)
        bias = b_ref[:, cs]
        for g0 in range(0, g, gc):
            acc = jnp.broadcast_to(bias, (gc, r, LANES))
            for k in range(taps):
                acc = acc + w_ref[k:k + 1, cs] * xs[g0:g0 + gc, off0 + k:off0 + k + r, cs]
            out[g0:g0 + gc, :, cs] = acc
        return carry

    lax.fori_loop(0, d // LANES, strip_rows if g == 1 else strip_groups, 0)


def _mixer_a_body(lay, d, taps, *refs):
    if lay.prompt:
        (x_ref, mod_ref, gpre, gpost, win, wdw, bdw, lng, lnb, wout,
         o_ref, tail_ref, xs, cv) = refs
        st_ref = None
    else:
        (x_ref, mod_ref, st_ref, gpre, gpost, win, wdw, bdw, lng, lnb, wout,
         o_ref, tail_ref, xs, cv) = refs
    g, r = lay.g, lay.r
    m = g * r
    t = pl.program_id(1)
    if lay.prompt:
        @pl.when(t == 0)
        def _():
            xs[:, 0:CONV_PAD, :] = jnp.zeros((g, CONV_PAD, d), F32)
            xs[:, r + CONV_PAD:, :] = jnp.zeros((g, SUBLANES, d), F32)
    else:
        xs[:, CONV_PAD - (taps - 1):CONV_PAD, :] = st_ref[...]

    x3 = x_ref[...]
    shift, scale, gate = mod_ref[:, 0:1, :], mod_ref[:, 1:2, :], mod_ref[:, 2:3, :]
    h3 = _rms(x3, gpre[...]) * (1.0 + scale) + shift
    ag = _dot(h3.reshape(m, d).astype(BF16), win[...])
    u = ag[:, :d] * jax.nn.sigmoid(ag[:, d:])
    xs[:, CONV_PAD:CONV_PAD + r, :] = u.reshape(g, r, d)
    _dwconv(xs, wdw, bdw, cv, g, r, d, taps)
    c3 = cv[...]
    mu = jnp.mean(c3, axis=-1, keepdims=True)
    dc = c3 - mu
    var = jnp.mean(dc * dc, axis=-1, keepdims=True)
    yn = dc * lax.rsqrt(var + EPS) * lng[...] + lnb[...]
    s = yn * jax.nn.sigmoid(yn)
    y = _dot(s.reshape(m, d).astype(BF16), wout[...])
    o_ref[...] = x3 + gate * _rms(y.reshape(g, r, d), gpost[...])

    lo = r + CONV_PAD - (taps - 1)
    if lay.prompt:
        @pl.when(t == lay.nt - 1)
        def _():
            tail_ref[...] = xs[:, lo:r + CONV_PAD, :]
        xs[:, 0:CONV_PAD, :] = xs[:, r:r + CONV_PAD, :]
    else:
        tail_ref[...] = xs[:, lo:r + CONV_PAD, :]


def _mixer_a(lay, l, x, mod, state, gpre, gpost, win, wdw, bdw, lng, lnb, wout):
    ng, rt, d = x.shape
    taps = wdw.shape[1]
    in_specs = [lay.act(d), lay.per_group(6, d, l)]
    args = [x, mod]
    if not lay.prompt:
        in_specs.append(lay.per_group(taps - 1, d, l))
        args.append(state)
    in_specs += [_const_spec((1, d), l), _const_spec((1, d), l), _const_spec((d, 2 * d), l),
                 _const_spec((taps, d), l), _const_spec((1, d), l), _const_spec((1, d), l),
                 _const_spec((1, d), l), _const_spec((d, d), l)]
    args += [gpre, gpost, win, wdw, bdw, lng, lnb, wout]
    return pl.pallas_call(
        functools.partial(_mixer_a_body, lay, d, taps),
        grid=lay.grid,
        in_specs=in_specs,
        out_specs=[lay.act(d), lay.per_group(taps - 1, d)],
        out_shape=[jax.ShapeDtypeStruct(x.shape, F32),
                   jax.ShapeDtypeStruct((ng, taps - 1, d), F32)],
        scratch_shapes=[pltpu.VMEM((lay.g, lay.r + CONV_PAD + (SUBLANES if lay.prompt else 0), d), F32),
                        pltpu.VMEM((lay.g, lay.r, d), F32)],
        compiler_params=_cparams("arbitrary", "arbitrary"),
        name="mixer_a_p" if lay.prompt else "mixer_a_s",
    )(*args)


def _ffn_body(lay, d, dff, chunks, *refs):
    if lay.prompt:
        (x_ref, mod_ref, gpre, gpost, wup, wdw, bdw, wdown,
         o_ref, tail_ref, gbuf, acc) = refs
        st_ref = None
    else:
        (x_ref, mod_ref, st_ref, gpre, gpost, wup, wdw, bdw, wdown,
         o_ref, tail_ref, gbuf, acc) = refs
    g, r = lay.g, lay.r
    m = g * r
    t = pl.program_id(1)
    if lay.prompt:
        @pl.when(t == 0)
        def _():
            gbuf[:, 0:SUBLANES, :] = jnp.zeros((g, SUBLANES, dff), F32)
    else:
        gbuf[:, SUBLANES - 2:SUBLANES, :] = st_ref[...]

    x3 = x_ref[...]
    shift, scale, gate = mod_ref[:, 3:4, :], mod_ref[:, 4:5, :], mod_ref[:, 5:6, :]
    hb = (_rms(x3, gpre[...]) * (1.0 + scale) + shift).reshape(m, d).astype(BF16)
    for c, (lo, fc) in enumerate(chunks):
        cs = slice(lo, lo + fc)
        gt = _dot(hb, wup[:, lo:lo + fc]).reshape(g, r, fc)
        val = _dot(hb, wup[:, dff + lo:dff + lo + fc]).reshape(g, r, fc)
        gbuf[:, SUBLANES:SUBLANES + r, cs] = gt
        g1 = gbuf[:, SUBLANES - 1:SUBLANES - 1 + r, cs]
        g2 = gbuf[:, SUBLANES - 2:SUBLANES - 2 + r, cs]
        y = wdw[0:1, cs] * g2 + wdw[1:2, cs] * g1 + wdw[2:3, cs] * gt + bdw[:, cs]
        act = 0.5 * y * (1.0 + jnp.tanh(GELU_C * (y + 0.044715 * (y * y * y)))) * val
        contrib = _dot(act.reshape(m, fc).astype(BF16), wdown[cs, :])
        if c == 0:
            acc[...] = contrib
        else:
            acc[...] += contrib
    o_ref[...] = x3 + gate * _rms(acc[...].reshape(g, r, d), gpost[...])

    if lay.prompt:
        @pl.when(t == lay.nt - 1)
        def _():
            tail_ref[...] = gbuf[:, r + SUBLANES - 2:r + SUBLANES, :]
        gbuf[:, 0:SUBLANES, :] = gbuf[:, r:r + SUBLANES, :]
    else:
        tail_ref[...] = gbuf[:, r + SUBLANES - 2:r + SUBLANES, :]


FFN_CHUNK = 768


def _ffn_chunks(dff):
    assert dff % 256 == 0
    return tuple((lo, min(FFN_CHUNK, dff - lo)) for lo in range(0, dff, FFN_CHUNK))


def _ffn(lay, l, x, mod, state, gpre, gpost, wup, wdw, bdw, wdown):
    ng, rt, d = x.shape
    dff = wdown.shape[1]
    chunks = _ffn_chunks(dff)
    in_specs = [lay.act(d), lay.per_group(6, d, l)]
    args = [x, mod]
    if not lay.prompt:
        in_specs.append(lay.per_group(2, dff, l))
        args.append(state)
    in_specs += [_const_spec((1, d), l), _const_spec((1, d), l), _const_spec((d, 2 * dff), l),
                 _const_spec((3, dff), l), _const_spec((1, dff), l), _const_spec((dff, d), l)]
    args += [gpre, gpost, wup, wdw, bdw, wdown]
    return pl.pallas_call(
        functools.partial(_ffn_body, lay, d, dff, chunks),
        grid=lay.grid,
        in_specs=in_specs,
        out_specs=[lay.act(d), lay.per_group(2, dff)],
        out_shape=[jax.ShapeDtypeStruct(x.shape, F32),
                   jax.ShapeDtypeStruct((ng, 2, dff), F32)],
        scratch_shapes=[pltpu.VMEM((lay.g, lay.r + SUBLANES, dff), F32),
                        pltpu.VMEM((lay.g * lay.r, d), F32)],
        compiler_params=_cparams("arbitrary", "arbitrary"),
        name="ffn_p" if lay.prompt else "ffn_s",
    )(*args)


def _kv_body(lay, d, hd, x_ref, g_ref, w_ref, k_ref, v_ref, *bf16_refs):
    m = lay.g * lay.r
    h = _rms(x_ref[...], g_ref[...]).reshape(m, d).astype(BF16)
    kv = _dot(h, w_ref[...])
    k = kv[:, :hd].reshape(lay.g, lay.r, hd)
    v = kv[:, hd:].reshape(lay.g, lay.r, hd)
    k_ref[...] = k
    v_ref[...] = v
    if bf16_refs:
        bf16_refs[0][...] = k.astype(BF16)
        bf16_refs[1][...] = v.astype(BF16)


def _kv_proj(lay, x, g_kv, w_kv):
    ng, rt, d = x.shape
    hd = w_kv.shape[1] // 2
    sd = jax.ShapeDtypeStruct
    out_shape = [sd((ng, rt, hd), F32), sd((ng, rt, hd), F32)]
    if lay.prompt:
        out_shape += [sd((ng, rt, hd), BF16), sd((ng, rt, hd), BF16)]
    return pl.pallas_call(
        functools.partial(_kv_body, lay, d, hd),
        grid=lay.grid,
        in_specs=[lay.act(d), _const_spec((1, d)), _const_spec((d, 2 * hd))],
        out_specs=[lay.act(hd)] * len(out_shape),
        out_shape=out_shape,
        compiler_params=_cparams("arbitrary", "arbitrary"),
        name="kv_p" if lay.prompt else "kv_s",
    )(x, g_kv, w_kv)


def _q_body(lay, d, hd, qscale, x_ref, mod_ref, gpre, w_ref, q_ref):
    m = lay.g * lay.r
    shift, scale = mod_ref[:, 0:1, :], mod_ref[:, 1:2, :]
    h = (_rms(x_ref[...], gpre[...]) * (1.0 + scale) + shift).reshape(m, d).astype(BF16)
    q = _dot(h, w_ref[...]) * qscale
    q_ref[...] = q.reshape(lay.g, lay.r, hd).astype(q_ref.dtype)


def _q_proj(lay, l, j, x, mod, gpre, w_q, qscale):
    ng, rt, d = x.shape
    hd = w_q.shape[2]
    return pl.pallas_call(
        functools.partial(_q_body, lay, d, hd, qscale),
        grid=lay.grid,
        in_specs=[lay.act(d), lay.per_group(6, d, l), _const_spec((1, d), l),
                  _const_spec((d, hd), j)],
        out_specs=lay.act(hd),
        out_shape=jax.ShapeDtypeStruct((ng, rt, hd), BF16 if lay.prompt else F32),
        compiler_params=_cparams("arbitrary", "arbitrary"),
        name="q_p" if lay.prompt else "q_s",
    )(x, mod, gpre, w_q)


def _o_body(lay, d, hd, o_ref, x_ref, mod_ref, gpost, w_ref, out_ref):
    m = lay.g * lay.r
    gate = mod_ref[:, 2:3, :]
    y = _dot(o_ref[...].reshape(m, hd).astype(BF16), w_ref[...])
    out_ref[...] = x_ref[...] + gate * _rms(y.reshape(lay.g, lay.r, d), gpost[...])


def _o_proj(lay, l, j, o, x, mod, gpost, w_o):
    ng, rt, d = x.shape
    hd = w_o.shape[1]
    return pl.pallas_call(
        functools.partial(_o_body, lay, d, hd),
        grid=lay.grid,
        in_specs=[lay.act(hd), lay.act(d), lay.per_group(6, d, l), _const_spec((1, d), l),
                  _const_spec((hd, d), j)],
        out_specs=lay.act(d),
        out_shape=jax.ShapeDtypeStruct(x.shape, F32),
        compiler_params=_cparams("arbitrary", "arbitrary"),
        name="o_p" if lay.prompt else "o_s",
    )(o, x, mod, gpost, w_o)


def _attn_p_body(tq, kb, dh, npair, bias_ref, q_ref, k_ref, v_ref, tri_ref, o_ref, acc):
    grp = pl.program_id(1)
    i = pl.program_id(2)
    nsub = tq // kb
    lane = lax.broadcasted_iota(jnp.int32, (tq, LANES), 1)
    tri = tri_ref[...]
    row = _imod(lax.broadcasted_iota(jnp.int32, (2 * tq, kb), 0), tq)
    col = lax.broadcasted_iota(jnp.int32, (2 * tq, kb), 1)
    qs, bias = [], []
    for p in range(npair):
        qp = q_ref[:, p * LANES:(p + 1) * LANES]
        zero = jnp.zeros_like(qp)
        qs.append(jnp.concatenate([jnp.where(lane < dh, qp, zero), jnp.where(lane >= dh, qp, zero)],
                                  axis=0))
        h0 = (grp * npair + p) * 2
        bias.append((bias_ref[h0] * LOG2E, bias_ref[h0 + 1] * LOG2E))

    def block(p, j, carry, sub):
        ks = pl.ds(pl.multiple_of(j * kb, kb), kb)
        ls = slice(p * LANES, (p + 1) * LANES)
        zz = _dot_nt(qs[p], k_ref[ks, ls])
        z = jnp.concatenate([zz[:tq] + bias[p][0], zz[tq:] + bias[p][1]], axis=0)
        s = _softplus2(z)
        if sub is not None:
            causal = col + sub * kb < row
            s = jnp.where(causal, s, 0.0)
        c = _dot(s, tri)
        e = z - c
        if carry is not None:
            e = e - carry
        att = jnp.exp2(e)
        if sub is not None:
            att = jnp.where(causal, att, 0.0)
        pv = _dot(att, v_ref[ks, ls])
        if carry is None:
            acc[p] = pv
            return c[:, 0:1]
        acc[p] += pv
        return carry + c[:, 0:1]

    carries = [None] * npair
    for sub in range(nsub - 1, -1, -1):
        for p in range(npair):
            carries[p] = block(p, i * nsub + sub, carries[p], sub)

    def step(s, carry):
        carry = list(carry)
        for u in range(nsub):
            for p in range(npair):
                carry[p] = block(p, (i - s) * nsub - 1 - u, carry[p], None)
        return tuple(carry)

    lax.fori_loop(0, i, step, tuple(carries))
    for p in range(npair):
        o_ref[:, p * LANES:(p + 1) * LANES] = jnp.where(
            lane < dh, acc[p, 0:tq, :], acc[p, tq:2 * tq, :]).astype(o_ref.dtype)


def _attn_prompt(q, kb_all, vb_all, bias, n_heads):
    b, t, hd = q.shape
    dh = hd // n_heads
    assert 2 * dh == LANES and n_heads % 2 == 0
    npair = 2 if n_heads % 4 == 0 else 1
    kb = min(256, t)
    tq = min(2 * kb, t)
    assert t % tq == 0 and tq % kb == 0
    width = npair * LANES
    tri = (jnp.arange(kb)[:, None] >= jnp.arange(kb)[None, :]).astype(BF16)
    return pl.pallas_call(
        functools.partial(_attn_p_body, tq, kb, dh, npair),
        grid=(b, n_heads // (2 * npair), t // tq),
        in_specs=[
            pl.BlockSpec(memory_space=pltpu.SMEM),
            pl.BlockSpec((None, tq, width), lambda bi, g, i: (bi, i, g)),
            pl.BlockSpec((None, t, width), lambda bi, g, i: (bi, 0, g)),
            pl.BlockSpec((None, t, width), lambda bi, g, i: (bi, 0, g)),
            pl.BlockSpec((kb, kb), lambda bi, g, i: (0, 0)),
        ],
        out_specs=pl.BlockSpec((None, tq, width), lambda bi, g, i: (bi, i, g)),
        out_shape=jax.ShapeDtypeStruct((b, t, hd), BF16),
        scratch_shapes=[pltpu.VMEM((npair, 2 * tq, LANES), F32)],
        compiler_params=_cparams("arbitrary", "arbitrary", "arbitrary"),
        name="attn_p",
    )(bias, q, kb_all, vb_all, tri)


def _gather_body(pg, page, hd, pt_ref, *refs):
    del pt_ref
    k_refs, v_refs = refs[:pg], refs[pg:2 * pg]
    kd_ref, vd_ref = refs[2 * pg:]
    for p in range(pg):
        rows = slice(p * page, (p + 1) * page)
        kd_ref[rows, :] = k_refs[p][...].reshape(page, hd).astype(BF16)
        vd_ref[rows, :] = v_refs[p][...].reshape(page, hd).astype(BF16)


def _gather_pages(cache_k, cache_v, page_table):
    _, page, n_heads, dh = cache_k.shape
    ns, n_pages = page_table.shape
    hd = n_heads * dh
    pg = 4 if n_pages % 4 == 0 else 1

    def page_spec(p):
        return pl.BlockSpec((None, page, n_heads, dh),
                            lambda s, c, pt: (pt[s * n_pages + c * pg + p], 0, 0, 0))

    out_spec = pl.BlockSpec((None, pg * page, hd), lambda s, c, pt: (s, c, 0))
    grid_spec = pltpu.PrefetchScalarGridSpec(
        num_scalar_prefetch=1,
        grid=(ns, n_pages // pg),
        in_specs=[page_spec(p) for p in range(pg)] * 2,
        out_specs=[out_spec, out_spec],
    )
    sd = jax.ShapeDtypeStruct((ns, n_pages * page, hd), BF16)
    return pl.pallas_call(
        functools.partial(_gather_body, pg, page, hd),
        grid_spec=grid_spec,
        out_shape=[sd, sd],
        compiler_params=_cparams("arbitrary", "arbitrary"),
        name="kv_gather",
    )(page_table.reshape(-1), *([cache_k] * pg), *([cache_v] * pg))


def _attn_s_body(kb, r, n_heads, dh, bias_ref, q_ref, kn_ref, vn_ref, tri_ref, kd_ref, vd_ref,
                 o_ref):
    past = kd_ref.shape[0]
    hd = n_heads * dh
    hq = n_heads * r
    row_h = _idiv(lax.broadcasted_iota(jnp.int32, (hq, hd), 0), r)
    lane_h = _idiv(lax.broadcasted_iota(jnp.int32, (hq, hd), 1), dh)
    own = row_h == lane_h
    q = q_ref[...]
    qrep = jnp.broadcast_to(q[None], (n_heads, r, hd)).reshape(hq, hd)
    qbd = jnp.where(own, qrep, 0.0).astype(BF16)
    rh = _idiv(lax.broadcasted_iota(jnp.int32, (hq, kb), 0), r)
    bias = jnp.zeros((hq, kb), F32)
    for h in range(n_heads):
        bias = jnp.where(rh == h, bias_ref[h] * LOG2E, bias)
    tri = tri_ref[...]

    qi = _imod(lax.broadcasted_iota(jnp.int32, (hq, LANES), 0), r)
    kj = lax.broadcasted_iota(jnp.int32, (hq, LANES), 1)
    causal = kj < qi
    pad = jnp.zeros((LANES - r, hd), F32)
    kn = jnp.concatenate([kn_ref[...], pad], axis=0).astype(BF16)
    vn = jnp.concatenate([vn_ref[...], pad], axis=0).astype(BF16)
    z = _dot_nt(qbd, kn) + bias[:, :LANES]
    s = jnp.where(causal, _softplus2(z), 0.0)
    c = _dot(s, tri[:LANES, :LANES])
    att_new = jnp.where(causal, jnp.exp2(z - c), 0.0)
    carry = c[:, 0:1]
    nb = past // kb
    zs = [_dot_nt(qbd, kd_ref[j * kb:(j + 1) * kb, :]) + bias for j in range(nb)]
    cs = [_dot(_softplus2(zj), tri) for zj in zs]
    atts = [None] * nb
    for j in range(nb - 1, -1, -1):
        atts[j] = jnp.exp2(zs[j] - cs[j] - carry)
        carry = carry + cs[j][:, 0:1]
    acc = _dot(att_new, vn) + _dot(jnp.concatenate(atts, axis=1), vd_ref[...])
    o = jnp.where(own, acc, 0.0).reshape(n_heads, r, hd)
    o_ref[...] = jnp.sum(o, axis=0).astype(o_ref.dtype)


def _attn_sample(q, k_new, v_new, kd, vd, bias, n_heads):
    ns, r, hd = q.shape
    dh = hd // n_heads
    past = kd.shape[1]
    kb = min(256, past)
    assert past % kb == 0 and r <= LANES <= kb
    tri = (jnp.arange(kb)[:, None] >= jnp.arange(kb)[None, :]).astype(BF16)
    seq = pl.BlockSpec((None, r, hd), lambda s: (s, 0, 0))
    dense = pl.BlockSpec((None, past, hd), lambda s: (s, 0, 0))
    return pl.pallas_call(
        functools.partial(_attn_s_body, kb, r, n_heads, dh),
        grid=(ns,),
        in_specs=[pl.BlockSpec(memory_space=pltpu.SMEM), seq, seq, seq,
                  pl.BlockSpec((kb, kb), lambda s: (0, 0)), dense, dense],
        out_specs=seq,
        out_shape=jax.ShapeDtypeStruct((ns, r, hd), F32),
        compiler_params=_cparams("arbitrary"),
        name="attn_s",
    )(bias, q, k_new, v_new, tri, kd, vd)


def kernel(x_prompt, x_sample, c_prompt, c_sample, state_conv_a, state_ffn_conv, cache_k, cache_v, page_table, w_ada, b_ada, g_pre_mix, g_post_mix, g_pre_ffn, g_post_ffn, w_a_in, w_a_dw, b_a_dw, ln_a_g, ln_a_b, w_a_out, g_kv, w_kv, w_q, w_o, b_sb, w_ffn_up, w_ffn_dw, b_ffn_dw, w_ffn_down):
    depth, d = g_pre_mix.shape
    n_a = w_a_in.shape[0]
    n_heads, dh = cache_k.shape[2], cache_k.shape[3]
    hd = n_heads * dh
    dff = w_ffn_down.shape[1]

    def vec(a):
        return a.reshape(a.shape[0], 1, a.shape[1])

    g_pre_mix, g_post_mix, g_pre_ffn, g_post_ffn = map(vec, (g_pre_mix, g_post_mix, g_pre_ffn, g_post_ffn))
    b_a_dw, ln_a_g, ln_a_b, b_ffn_dw = map(vec, (b_a_dw, ln_a_g, ln_a_b, b_ffn_dw))
    w_a_in, w_a_out, w_kv, w_q, w_o, w_ffn_up, w_ffn_down = (
        w.astype(BF16) for w in (w_a_in, w_a_out, w_kv, w_q, w_o, w_ffn_up, w_ffn_down))
    g_kv = g_kv.reshape(1, d)

    mod_p, mod_s = _ada(c_prompt, c_sample, w_ada, b_ada)
    past_kv = _gather_pages(cache_k, cache_v, page_table)
    groups = [
        dict(x=x_prompt, mod=mod_p.reshape(depth, -1, 6, d), conv_state=None, ffn_state=None,
             lay_a=_Layout(x_prompt.shape, True, 512, 0), lay_f=_Layout(x_prompt.shape, True, 512, 0),
             lay_d=_Layout(x_prompt.shape, True, 512, 0)),
        dict(x=x_sample, mod=mod_s.reshape(depth, -1, 6, d), conv_state=state_conv_a,
             ffn_state=state_ffn_conv,
             lay_a=_Layout(x_sample.shape, False, 0, 32), lay_f=_Layout(x_sample.shape, False, 0, 32),
             lay_d=_Layout(x_sample.shape, False, 0, 64)),
    ]
    results = []
    for grp in groups:
        x, mod = grp["x"], grp["mod"]
        prompt = grp["lay_a"].prompt
        conv_tails, ffn_tails = [], []
        for l in range(depth):
            if l < n_a:
                x, tail = _mixer_a(grp["lay_a"], l, x, mod, grp["conv_state"], g_pre_mix, g_post_mix,
                                   w_a_in, w_a_dw, b_a_dw, ln_a_g, ln_a_b, w_a_out)
                conv_tails.append(tail)
            else:
                j = l - n_a
                q = _q_proj(grp["lay_d"], l, j, x, mod, g_pre_mix, w_q, dh ** -0.5 * LOG2E)
                if prompt:
                    o = _attn_prompt(q, *kv_bf16, b_sb[j], n_heads)
                else:
                    o = _attn_sample(q, k_new, v_new, *past_kv, b_sb[j], n_heads)
                x = _o_proj(grp["lay_d"], l, j, o, x, mod, g_post_mix, w_o)
            x, tail = _ffn(grp["lay_f"], l, x, mod, grp["ffn_state"], g_pre_ffn, g_post_ffn,
                           w_ffn_up, w_ffn_dw, b_ffn_dw, w_ffn_down)
            ffn_tails.append(tail)
            if l == n_a - 1:
                k_new, v_new, *kv_bf16 = _kv_proj(grp["lay_d"], x, g_kv, w_kv)
        shp = x.shape[:2] + (n_heads, dh)
        results.append((x, jnp.stack(conv_tails), jnp.stack(ffn_tails),
                        k_new.reshape(shp), v_new.reshape(shp)))
    (yp, cap, ffp, kp, vp), (ys, cas, ffs, ks, vs) = results
    return (yp, ys, cap, cas, ffp, ffs, kp, vp, ks, vs)
```

```python
import functools

import jax
import jax.numpy as jnp
from jax import lax
from jax.experimental import pallas as pl
from jax.experimental.pallas import tpu as pltpu

F32 = jnp.float32
BF16 = jnp.bfloat16
EPS = 1e-6
LANES = 128
SUBLANES = 8
CONV_PAD = 32
VMEM_LIMIT = 56 * 1024 * 1024
GELU_C = 0.7978845608028654
LOG2E = 1.4426950408889634
SIGN_BIT = 0x80000000


def _cparams(*sem):
    return pltpu.CompilerParams(dimension_semantics=sem, vmem_limit_bytes=VMEM_LIMIT)


def _const_spec(shape, layer=None):
    nd = len(shape)
    if layer is None:
        return pl.BlockSpec(shape, lambda *_: (0,) * nd, pipeline_mode=pl.Buffered(1))
    return pl.BlockSpec((None,) + tuple(shape), lambda *_: (layer,) + (0,) * nd,
                        pipeline_mode=pl.Buffered(1))


def _pick_tile(n, target):
    best = None
    for t in range(LANES, min(n, target) + 1, LANES):
        if n % t == 0:
            best = t
    assert best is not None
    return best


def _rms(x, g):
    ms = jnp.mean(x * x, axis=-1, keepdims=True)
    return x * lax.rsqrt(ms + EPS) * g


def _dot(a, b):
    return jnp.dot(a, b, preferred_element_type=F32)


def _dot_nt(a, b):
    return lax.dot_general(a, b, (((1,), (1,)), ((), ())), preferred_element_type=F32)


def _softplus2(z):
    bits = lax.bitcast_convert_type(z, jnp.uint32) | jnp.uint32(SIGN_BIT)
    neg_abs = lax.bitcast_convert_type(bits, F32)
    return jnp.maximum(z, 0.0) + jnp.log2(1.0 + jnp.exp2(neg_abs))


def _idiv(x, n):
    assert n & (n - 1) == 0, "power-of-two divisor"
    return lax.shift_right_logical(x, n.bit_length() - 1)


def _imod(x, n):
    assert n & (n - 1) == 0, "power-of-two divisor"
    return x & (n - 1)


def _ada_body(cp_ref, cs_ref, w_ref, b_ref, op_ref, os_ref):
    w = w_ref[...].astype(BF16)
    b = b_ref[...]
    for c_ref, o_ref in ((cp_ref, op_ref), (cs_ref, os_ref)):
        c = c_ref[...]
        a = (c * jax.nn.sigmoid(c)).astype(BF16)
        o_ref[...] = _dot(a, w) + b


def _ada(c_prompt, c_sample, w_ada, b_ada):
    depth, d, n = w_ada.shape
    tn = _pick_tile(n, 1024)
    bp, bs = c_prompt.shape[0], c_sample.shape[0]
    return pl.pallas_call(
        _ada_body,
        grid=(depth, n // tn),
        in_specs=[
            pl.BlockSpec((bp, d), lambda l, j: (0, 0)),
            pl.BlockSpec((bs, d), lambda l, j: (0, 0)),
            pl.BlockSpec((None, d, tn), lambda l, j: (l, 0, j)),
            pl.BlockSpec((None, 1, tn), lambda l, j: (l, 0, j)),
        ],
        out_specs=[
            pl.BlockSpec((None, bp, tn), lambda l, j: (l, 0, j)),
            pl.BlockSpec((None, bs, tn), lambda l, j: (l, 0, j)),
        ],
        out_shape=[jax.ShapeDtypeStruct((depth, bp, n), F32),
                   jax.ShapeDtypeStruct((depth, bs, n), F32)],
        compiler_params=_cparams("arbitrary", "arbitrary"),
        name="ada",
    )(c_prompt, c_sample, w_ada, b_ada.reshape(depth, 1, n))


class _Layout:
    def __init__(self, x_shape, prompt, row_tile, seq_tile):
        ng, rt, _ = x_shape
        self.prompt = prompt
        if prompt:
            self.g, self.r = 1, min(row_tile, rt)
            assert rt % self.r == 0
            self.grid = (ng, rt // self.r)
        else:
            self.g, self.r = min(seq_tile, ng), rt
            assert ng % self.g == 0
            self.grid = (ng // self.g, 1)
        self.nt = self.grid[1]

    def act(self, d):
        return pl.BlockSpec((self.g, self.r, d), lambda i, t: (i, t, 0))

    def per_group(self, rows, d, layer=None):
        if layer is None:
            return pl.BlockSpec((self.g, rows, d), lambda i, t: (i, 0, 0))
        return pl.BlockSpec((None, self.g, rows, d), lambda i, t: (layer, i, 0, 0))


def _dwconv(xs, w_ref, b_ref, out, g, r, d, taps):
    off0 = CONV_PAD - (taps - 1)
    gc = min(g, 4)
    rc = min(r, 64)

    def strip_rows(j, carry):
        cs = pl.ds(pl.multiple_of(j * LANES, LANES), LANES)
        bias = b_ref[:, cs]
        for r0 in range(0, r, rc):
            acc = jnp.broadcast_to(bias, (rc, LANES))
            for res in range(SUBLANES):
                z = None
                for k in range(taps):
                    if (k + off0) % SUBLANES != res:
                        continue
                    base = r0 + k + off0 - res
                    term = w_ref[k:k + 1, cs] * xs[0, base:base + rc + SUBLANES, cs]
                    z = term if z is None else z + term
                if z is not None:
                    acc = acc + z[res:res + rc]
            out[0, r0:r0 + rc, cs] = acc
        return carry

    def strip_groups(j, carry):
        cs = pl.ds(pl.multiple_of(j * LANES, LANES), LANES)
        bias = b_ref[:, cs]
        for g0 in range(0, g, gc):
            acc = jnp.broadcast_to(bias, (gc, r, LANES))
            for k in range(taps):
                acc = acc + w_ref[k:k + 1, cs] * xs[g0:g0 + gc, off0 + k:off0 + k + r, cs]
            out[g0:g0 + gc, :, cs] = acc
        return carry

    lax.fori_loop(0, d // LANES, strip_rows if g == 1 else strip_groups, 0)


def _mixer_a_body(lay, d, taps, *refs):
    if lay.prompt:
        (x_ref, mod_ref, gpre, gpost, win, wdw, bdw, lng, lnb, wout,
         o_ref, tail_ref, xs, cv) = refs
        st_ref = None
    else:
        (x_ref, mod_ref, st_ref, gpre, gpost, win, wdw, bdw, lng, lnb, wout,
         o_ref, tail_ref, xs, cv) = refs
    g, r = lay.g, lay.r
    m = g * r
    t = pl.program_id(1)
    if lay.prompt:
        @pl.when(t == 0)
        def _():
            xs[:, 0:CONV_PAD, :] = jnp.zeros((g, CONV_PAD, d), F32)
            xs[:, r + CONV_PAD:, :] = jnp.zeros((g, SUBLANES, d), F32)
    else:
        xs[:, CONV_PAD - (taps - 1):CONV_PAD, :] = st_ref[...]

    x3 = x_ref[...]
    shift, scale, gate = mod_ref[:, 0:1, :], mod_ref[:, 1:2, :], mod_ref[:, 2:3, :]
    h3 = _rms(x3, gpre[...]) * (1.0 + scale) + shift
    ag = _dot(h3.reshape(m, d).astype(BF16), win[...])
    u = ag[:, :d] * jax.nn.sigmoid(ag[:, d:])
    xs[:, CONV_PAD:CONV_PAD + r, :] = u.reshape(g, r, d)
    _dwconv(xs, wdw, bdw, cv, g, r, d, taps)
    c3 = cv[...]
    mu = jnp.mean(c3, axis=-1, keepdims=True)
    dc = c3 - mu
    var = jnp.mean(dc * dc, axis=-1, keepdims=True)
    yn = dc * lax.rsqrt(var + EPS) * lng[...] + lnb[...]
    s = yn * jax.nn.sigmoid(yn)
    y = _dot(s.reshape(m, d).astype(BF16), wout[...])
    o_ref[...] = x3 + gate * _rms(y.reshape(g, r, d), gpost[...])

    lo = r + CONV_PAD - (taps - 1)
    if lay.prompt:
        @pl.when(t == lay.nt - 1)
        def _():
            tail_ref[...] = xs[:, lo:r + CONV_PAD, :]
        xs[:, 0:CONV_PAD, :] = xs[:, r:r + CONV_PAD, :]
    else:
        tail_ref[...] = xs[:, lo:r + CONV_PAD, :]


def _mixer_a(lay, l, x, mod, state, gpre, gpost, win, wdw, bdw, lng, lnb, wout):
    ng, rt, d = x.shape
    taps = wdw.shape[1]
    in_specs = [lay.act(d), lay.per_group(6, d, l)]
    args = [x, mod]
    if not lay.prompt:
        in_specs.append(lay.per_group(taps - 1, d, l))
        args.append(state)
    in_specs += [_const_spec((1, d), l), _const_spec((1, d), l), _const_spec((d, 2 * d), l),
                 _const_spec((taps, d), l), _const_spec((1, d), l), _const_spec((1, d), l),
                 _const_spec((1, d), l), _const_spec((d, d), l)]
    args += [gpre, gpost, win, wdw, bdw, lng, lnb, wout]
    return pl.pallas_call(
        functools.partial(_mixer_a_body, lay, d, taps),
        grid=lay.grid,
        in_specs=in_specs,
        out_specs=[lay.act(d), lay.per_group(taps - 1, d)],
        out_shape=[jax.ShapeDtypeStruct(x.shape, F32),
                   jax.ShapeDtypeStruct((ng, taps - 1, d), F32)],
        scratch_shapes=[pltpu.VMEM((lay.g, lay.r + CONV_PAD + (SUBLANES if lay.prompt else 0), d), F32),
                        pltpu.VMEM((lay.g, lay.r, d), F32)],
        compiler_params=_cparams("arbitrary", "arbitrary"),
        name="mixer_a_p" if lay.prompt else "mixer_a_s",
    )(*args)


def _ffn_body(lay, d, dff, chunks, *refs):
    if lay.prompt:
        (x_ref, mod_ref, gpre, gpost, wup, wdw, bdw, wdown,
         o_ref, tail_ref, gbuf, acc) = refs
        st_ref = None
    else:
        (x_ref, mod_ref, st_ref, gpre, gpost, wup, wdw, bdw, wdown,
         o_ref, tail_ref, gbuf, acc) = refs
    g, r = lay.g, lay.r
    m = g * r
    t = pl.program_id(1)
    if lay.prompt:
        @pl.when(t == 0)
        def _():
            gbuf[:, 0:SUBLANES, :] = jnp.zeros((g, SUBLANES, dff), F32)
    else:
        gbuf[:, SUBLANES - 2:SUBLANES, :] = st_ref[...]

    x3 = x_ref[...]
    shift, scale, gate = mod_ref[:, 3:4, :], mod_ref[:, 4:5, :], mod_ref[:, 5:6, :]
    hb = (_rms(x3, gpre[...]) * (1.0 + scale) + shift).reshape(m, d).astype(BF16)
    for c, (lo, fc) in enumerate(chunks):
        cs = slice(lo, lo + fc)
        gt = _dot(hb, wup[:, lo:lo + fc]).reshape(g, r, fc)
        val = _dot(hb, wup[:, dff + lo:dff + lo + fc]).reshape(g, r, fc)
        gbuf[:, SUBLANES:SUBLANES + r, cs] = gt
        g1 = gbuf[:, SUBLANES - 1:SUBLANES - 1 + r, cs]
        g2 = gbuf[:, SUBLANES - 2:SUBLANES - 2 + r, cs]
        y = wdw[0:1, cs] * g2 + wdw[1:2, cs] * g1 + wdw[2:3, cs] * gt + bdw[:, cs]
        act = 0.5 * y * (1.0 + jnp.tanh(GELU_C * (y + 0.044715 * (y * y * y)))) * val
        contrib = _dot(act.reshape(m, fc).astype(BF16), wdown[cs, :])
        if c == 0:
            acc[...] = contrib
        else:
            acc[...] += contrib
    o_ref[...] = x3 + gate * _rms(acc[...].reshape(g, r, d), gpost[...])

    if lay.prompt:
        @pl.when(t == lay.nt - 1)
        def _():
            tail_ref[...] = gbuf[:, r + SUBLANES - 2:r + SUBLANES, :]
        gbuf[:, 0:SUBLANES, :] = gbuf[:, r:r + SUBLANES, :]
    else:
        tail_ref[...] = gbuf[:, r + SUBLANES - 2:r + SUBLANES, :]


FFN_CHUNK = 768


def _ffn_chunks(dff):
    assert dff % 256 == 0
    return tuple((lo, min(FFN_CHUNK, dff - lo)) for lo in range(0, dff, FFN_CHUNK))


def _ffn(lay, l, x, mod, state, gpre, gpost, wup, wdw, bdw, wdown):
    ng, rt, d = x.shape
    dff = wdown.shape[1]
    chunks = _ffn_chunks(dff)
    in_specs = [lay.act(d), lay.per_group(6, d, l)]
    args = [x, mod]
    if not lay.prompt:
        in_specs.append(lay.per_group(2, dff, l))
        args.append(state)
    in_specs += [_const_spec((1, d), l), _const_spec((1, d), l), _const_spec((d, 2 * dff), l),
                 _const_spec((3, dff), l), _const_spec((1, dff), l), _const_spec((dff, d), l)]
    args += [gpre, gpost, wup, wdw, bdw, wdown]
    return pl.pallas_call(
        functools.partial(_ffn_body, lay, d, dff, chunks),
        grid=lay.grid,
        in_specs=in_specs,
        out_specs=[lay.act(d), lay.per_group(2, dff)],
        out_shape=[jax.ShapeDtypeStruct(x.shape, F32),
                   jax.ShapeDtypeStruct((ng, 2, dff), F32)],
        scratch_shapes=[pltpu.VMEM((lay.g, lay.r + SUBLANES, dff), F32),
                        pltpu.VMEM((lay.g * lay.r, d), F32)],
        compiler_params=_cparams("arbitrary", "arbitrary"),
        name="ffn_p" if lay.prompt else "ffn_s",
    )(*args)


def _kv_body(lay, d, hd, x_ref, g_ref, w_ref, k_ref, v_ref, *bf16_refs):
    m = lay.g * lay.r
    h = _rms(x_ref[...], g_ref[...]).reshape(m, d).astype(BF16)
    kv = _dot(h, w_ref[...])
    k = kv[:, :hd].reshape(lay.g, lay.r, hd)
    v = kv[:, hd:].reshape(lay.g, lay.r, hd)
    k_ref[...] = k
    v_ref[...] = v
    if bf16_refs:
        bf16_refs[0][...] = k.astype(BF16)
        bf16_refs[1][...] = v.astype(BF16)


def _kv_proj(lay, x, g_kv, w_kv):
    ng, rt, d = x.shape
    hd = w_kv.shape[1] // 2
    sd = jax.ShapeDtypeStruct
    out_shape = [sd((ng, rt, hd), F32), sd((ng, rt, hd), F32)]
    if lay.prompt:
        out_shape += [sd((ng, rt, hd), BF16), sd((ng, rt, hd), BF16)]
    return pl.pallas_call(
        functools.partial(_kv_body, lay, d, hd),
        grid=lay.grid,
        in_specs=[lay.act(d), _const_spec((1, d)), _const_spec((d, 2 * hd))],
        out_specs=[lay.act(hd)] * len(out_shape),
        out_shape=out_shape,
        compiler_params=_cparams("arbitrary", "arbitrary"),
        name="kv_p" if lay.prompt else "kv_s",
    )(x, g_kv, w_kv)


def _q_body(lay, d, hd, qscale, x_ref, mod_ref, gpre, w_ref, q_ref):
    m = lay.g * lay.r
    shift, scale = mod_ref[:, 0:1, :], mod_ref[:, 1:2, :]
    h = (_rms(x_ref[...], gpre[...]) * (1.0 + scale) + shift).reshape(m, d).astype(BF16)
    q = _dot(h, w_ref[...]) * qscale
    q_ref[...] = q.reshape(lay.g, lay.r, hd).astype(q_ref.dtype)


def _q_proj(lay, l, j, x, mod, gpre, w_q, qscale):
    ng, rt, d = x.shape
    hd = w_q.shape[2]
    return pl.pallas_call(
        functools.partial(_q_body, lay, d, hd, qscale),
        grid=lay.grid,
        in_specs=[lay.act(d), lay.per_group(6, d, l), _const_spec((1, d), l),
                  _const_spec((d, hd), j)],
        out_specs=lay.act(hd),
        out_shape=jax.ShapeDtypeStruct((ng, rt, hd), BF16 if lay.prompt else F32),
        compiler_params=_cparams("arbitrary", "arbitrary"),
        name="q_p" if lay.prompt else "q_s",
    )(x, mod, gpre, w_q)


def _o_body(lay, d, hd, o_ref, x_ref, mod_ref, gpost, w_ref, out_ref):
    m = lay.g * lay.r
    gate = mod_ref[:, 2:3, :]
    y = _dot(o_ref[...].reshape(m, hd).astype(BF16), w_ref[...])
    out_ref[...] = x_ref[...] + gate * _rms(y.reshape(lay.g, lay.r, d), gpost[...])


def _o_proj(lay, l, j, o, x, mod, gpost, w_o):
    ng, rt, d = x.shape
    hd = w_o.shape[1]
    return pl.pallas_call(
        functools.partial(_o_body, lay, d, hd),
        grid=lay.grid,
        in_specs=[lay.act(hd), lay.act(d), lay.per_group(6, d, l), _const_spec((1, d), l),
                  _const_spec((hd, d), j)],
        out_specs=lay.act(d),
        out_shape=jax.ShapeDtypeStruct(x.shape, F32),
        compiler_params=_cparams("arbitrary", "arbitrary"),
        name="o_p" if lay.prompt else "o_s",
    )(o, x, mod, gpost, w_o)


def _attn_p_body(tq, kb, dh, npair, bias_ref, q_ref, k_ref, v_ref, tri_ref, o_ref, acc):
    grp = pl.program_id(1)
    i = pl.program_id(2)
    nsub = tq // kb
    lane = lax.broadcasted_iota(jnp.int32, (tq, LANES), 1)
    tri = tri_ref[...]
    row = _imod(lax.broadcasted_iota(jnp.int32, (2 * tq, kb), 0), tq)
    col = lax.broadcasted_iota(jnp.int32, (2 * tq, kb), 1)
    qs, bias = [], []
    for p in range(npair):
        qp = q_ref[:, p * LANES:(p + 1) * LANES]
        zero = jnp.zeros_like(qp)
        qs.append(jnp.concatenate([jnp.where(lane < dh, qp, zero), jnp.where(lane >= dh, qp, zero)],
                                  axis=0))
        h0 = (grp * npair + p) * 2
        bias.append((bias_ref[h0] * LOG2E, bias_ref[h0 + 1] * LOG2E))

    def block(p, j, carry, sub):
        ks = pl.ds(pl.multiple_of(j * kb, kb), kb)
        ls = slice(p * LANES, (p + 1) * LANES)
        zz = _dot_nt(qs[p], k_ref[ks, ls])
        z = jnp.concatenate([zz[:tq] + bias[p][0], zz[tq:] + bias[p][1]], axis=0)
        s = _softplus2(z)
        if sub is not None:
            causal = col + sub * kb < row
            s = jnp.where(causal, s, 0.0)
        c = _dot(s, tri)
        e = z - c
        if carry is not None:
            e = e - carry
        att = jnp.exp2(e)
        if sub is not None:
            att = jnp.where(causal, att, 0.0)
        pv = _dot(att, v_ref[ks, ls])
        if carry is None:
            acc[p] = pv
            return c[:, 0:1]
        acc[p] += pv
        return carry + c[:, 0:1]

    carries = [None] * npair
    for sub in range(nsub - 1, -1, -1):
        for p in range(npair):
            carries[p] = block(p, i * nsub + sub, carries[p], sub)

    def step(s, carry):
        carry = list(carry)
        for u in range(nsub):
            for p in range(npair):
                carry[p] = block(p, (i - s) * nsub - 1 - u, carry[p], None)
        return tuple(carry)

    lax.fori_loop(0, i, step, tuple(carries))
    for p in range(npair):
        o_ref[:, p * LANES:(p + 1) * LANES] = jnp.where(
            lane < dh, acc[p, 0:tq, :], acc[p, tq:2 * tq, :]).astype(o_ref.dtype)


def _attn_prompt(q, kb_all, vb_all, bias, n_heads):
    b, t, hd = q.shape
    dh = hd // n_heads
    assert 2 * dh == LANES and n_heads % 2 == 0
    npair = 2 if n_heads % 4 == 0 else 1
    kb = min(256, t)
    tq = min(2 * kb, t)
    assert t % tq == 0 and tq % kb == 0
    width = npair * LANES
    tri = (jnp.arange(kb)[:, None] >= jnp.arange(kb)[None, :]).astype(BF16)
    return pl.pallas_call(
        functools.partial(_attn_p_body, tq, kb, dh, npair),
        grid=(b, n_heads // (2 * npair), t // tq),
        in_specs=[
            pl.BlockSpec(memory_space=pltpu.SMEM),
            pl.BlockSpec((None, tq, width), lambda bi, g, i: (bi, i, g)),
            pl.BlockSpec((None, t, width), lambda bi, g, i: (bi, 0, g)),
            pl.BlockSpec((None, t, width), lambda bi, g, i: (bi, 0, g)),
            pl.BlockSpec((kb, kb), lambda bi, g, i: (0, 0)),
        ],
        out_specs=pl.BlockSpec((None, tq, width), lambda bi, g, i: (bi, i, g)),
        out_shape=jax.ShapeDtypeStruct((b, t, hd), BF16),
        scratch_shapes=[pltpu.VMEM((npair, 2 * tq, LANES), F32)],
        compiler_params=_cparams("arbitrary", "arbitrary", "arbitrary"),
        name="attn_p",
    )(bias, q, kb_all, vb_all, tri)


def _attn_s_body(n_pages, ppb, page, r, n_heads, dh, pt_ref, bias_ref, q_ref, kn_ref, vn_ref,
                 tri_ref, *refs):
    del pt_ref
    k_refs, v_refs = refs[:n_pages], refs[n_pages:2 * n_pages]
    o_ref = refs[2 * n_pages]
    kb = ppb * page
    hd = n_heads * dh
    hq = n_heads * r
    row_h = _idiv(lax.broadcasted_iota(jnp.int32, (hq, hd), 0), r)
    lane_h = _idiv(lax.broadcasted_iota(jnp.int32, (hq, hd), 1), dh)
    own = row_h == lane_h
    q = q_ref[...]
    qrep = jnp.broadcast_to(q[None], (n_heads, r, hd)).reshape(hq, hd)
    qbd = jnp.where(own, qrep, 0.0).astype(BF16)
    rh = _idiv(lax.broadcasted_iota(jnp.int32, (hq, kb), 0), r)
    bias = jnp.zeros((hq, kb), F32)
    for h in range(n_heads):
        bias = jnp.where(rh == h, bias_ref[h] * LOG2E, bias)
    tri = tri_ref[...]

    qi = _imod(lax.broadcasted_iota(jnp.int32, (hq, LANES), 0), r)
    kj = lax.broadcasted_iota(jnp.int32, (hq, LANES), 1)
    causal = kj < qi
    pad = jnp.zeros((LANES - r, hd), F32)
    kn = jnp.concatenate([kn_ref[...], pad], axis=0).astype(BF16)
    vn = jnp.concatenate([vn_ref[...], pad], axis=0).astype(BF16)
    z = _dot_nt(qbd, kn) + bias[:, :LANES]
    s = jnp.where(causal, _softplus2(z), 0.0)
    c = _dot(s, tri[:LANES, :LANES])
    att_new = jnp.where(causal, jnp.exp2(z - c), 0.0)
    carry = c[:, 0:1]
    nb = n_pages // ppb

    def block_t(refs, j):
        pages = [refs[j * ppb + u][...] for u in range(ppb)]
        return jnp.concatenate(pages, axis=1).astype(BF16)

    zs = [_dot(qbd, block_t(k_refs, j)) + bias for j in range(nb)]
    cs = [_dot(_softplus2(zj), tri) for zj in zs]
    acc = _dot(att_new, vn)
    for j in range(nb - 1, -1, -1):
        att = jnp.exp2(zs[j] - cs[j] - carry)
        carry = carry + cs[j][:, 0:1]
        acc = acc + _dot_nt(att, block_t(v_refs, j))
    o = jnp.where(own, acc, 0.0).reshape(n_heads, r, hd)
    o_ref[...] = jnp.sum(o, axis=0).astype(o_ref.dtype)


def _attn_sample(q, k_new, v_new, cache_kt, cache_vt, page_table, bias, n_heads):
    ns, r, hd = q.shape
    dh = hd // n_heads
    page = cache_kt.shape[2]
    n_pages = page_table.shape[1]
    ppb = max(1, 256 // page)
    kb = ppb * page
    assert n_pages % ppb == 0 and page % LANES == 0 and r <= LANES
    tri = (jnp.arange(kb)[:, None] >= jnp.arange(kb)[None, :]).astype(BF16)
    seq = pl.BlockSpec((None, r, hd), lambda s, pt: (s, 0, 0))

    def page_spec(p):
        return pl.BlockSpec((None, hd, page), lambda s, pt: (pt[s * n_pages + p], 0, 0))

    grid_spec = pltpu.PrefetchScalarGridSpec(
        num_scalar_prefetch=1,
        grid=(ns,),
        in_specs=[pl.BlockSpec(memory_space=pltpu.SMEM), seq, seq, seq,
                  pl.BlockSpec((kb, kb), lambda s, pt: (0, 0))]
        + [page_spec(p) for p in range(n_pages)] * 2,
        out_specs=seq,
    )
    return pl.pallas_call(
        functools.partial(_attn_s_body, n_pages, ppb, page, r, n_heads, dh),
        grid_spec=grid_spec,
        out_shape=jax.ShapeDtypeStruct((ns, r, hd), F32),
        compiler_params=_cparams("arbitrary"),
        name="attn_s",
    )(page_table.reshape(-1), bias, q, k_new, v_new, tri,
      *([cache_kt] * n_pages), *([cache_vt] * n_pages))


def kernel(x_prompt, x_sample, c_prompt, c_sample, state_conv_a, state_ffn_conv, cache_k, cache_v, page_table, w_ada, b_ada, g_pre_mix, g_post_mix, g_pre_ffn, g_post_ffn, w_a_in, w_a_dw, b_a_dw, ln_a_g, ln_a_b, w_a_out, g_kv, w_kv, w_q, w_o, b_sb, w_ffn_up, w_ffn_dw, b_ffn_dw, w_ffn_down):
    depth, d = g_pre_mix.shape
    n_a = w_a_in.shape[0]
    n_heads, dh = cache_k.shape[2], cache_k.shape[3]
    hd = n_heads * dh
    dff = w_ffn_down.shape[1]

    def vec(a):
        return a.reshape(a.shape[0], 1, a.shape[1])

    g_pre_mix, g_post_mix, g_pre_ffn, g_post_ffn = map(vec, (g_pre_mix, g_post_mix, g_pre_ffn, g_post_ffn))
    b_a_dw, ln_a_g, ln_a_b, b_ffn_dw = map(vec, (b_a_dw, ln_a_g, ln_a_b, b_ffn_dw))
    w_a_in, w_a_out, w_kv, w_q, w_o, w_ffn_up, w_ffn_down = (
        w.astype(BF16) for w in (w_a_in, w_a_out, w_kv, w_q, w_o, w_ffn_up, w_ffn_down))
    g_kv = g_kv.reshape(1, d)

    mod_p, mod_s = _ada(c_prompt, c_sample, w_ada, b_ada)
    n_phys, page = cache_k.shape[:2]
    past_kv = tuple(c.transpose(0, 2, 3, 1).reshape(n_phys, hd, page) for c in (cache_k, cache_v))
    groups = [
        dict(x=x_prompt, mod=mod_p.reshape(depth, -1, 6, d), conv_state=None, ffn_state=None,
             lay_a=_Layout(x_prompt.shape, True, 512, 0), lay_f=_Layout(x_prompt.shape, True, 512, 0),
             lay_d=_Layout(x_prompt.shape, True, 512, 0)),
        dict(x=x_sample, mod=mod_s.reshape(depth, -1, 6, d), conv_state=state_conv_a,
             ffn_state=state_ffn_conv,
             lay_a=_Layout(x_sample.shape, False, 0, 32), lay_f=_Layout(x_sample.shape, False, 0, 32),
             lay_d=_Layout(x_sample.shape, False, 0, 64)),
    ]
    results = []
    for grp in groups:
        x, mod = grp["x"], grp["mod"]
        prompt = grp["lay_a"].prompt
        conv_tails, ffn_tails = [], []
        for l in range(depth):
            if l < n_a:
                x, tail = _mixer_a(grp["lay_a"], l, x, mod, grp["conv_state"], g_pre_mix, g_post_mix,
                                   w_a_in, w_a_dw, b_a_dw, ln_a_g, ln_a_b, w_a_out)
                conv_tails.append(tail)
            else:
                j = l - n_a
                q = _q_proj(grp["lay_d"], l, j, x, mod, g_pre_mix, w_q, dh ** -0.5 * LOG2E)
                if prompt:
                    o = _attn_prompt(q, *kv_bf16, b_sb[j], n_heads)
                else:
                    o = _attn_sample(q, k_new, v_new, *past_kv, page_table, b_sb[j], n_heads)
                x = _o_proj(grp["lay_d"], l, j, o, x, mod, g_post_mix, w_o)
            x, tail = _ffn(grp["lay_f"], l, x, mod, grp["ffn_state"], g_pre_ffn, g_post_ffn,
                           w_ffn_up, w_ffn_dw, b_ffn_dw, w_ffn_down)
            ffn_tails.append(tail)
            if l == n_a - 1:
                k_new, v_new, *kv_bf16 = _kv_proj(grp["lay_d"], x, g_kv, w_kv)
        shp = x.shape[:2] + (n_heads, dh)
        results.append((x, jnp.stack(conv_tails), jnp.stack(ffn_tails),
                        k_new.reshape(shp), v_new.reshape(shp)))
    (yp, cap, ffp, kp, vp), (ys, cas, ffs, ks, vs) = results
    return (yp, ys, cap, cas, ffp, ffs, kp, vp, ks, vs)
```

```python
import functools

import jax
import jax.numpy as jnp
from jax import lax
from jax.experimental import pallas as pl
from jax.experimental.pallas import tpu as pltpu

F32 = jnp.float32
BF16 = jnp.bfloat16
EPS = 1e-6
LANES = 128
SUBLANES = 8
CONV_PAD = 32
VMEM_LIMIT = 56 * 1024 * 1024
GELU_C = 0.7978845608028654
LOG2E = 1.4426950408889634
SIGN_BIT = 0x80000000


def _cparams(*sem):
    return pltpu.CompilerParams(dimension_semantics=sem, vmem_limit_bytes=VMEM_LIMIT)


def _const_spec(shape, layer=None):
    nd = len(shape)
    if layer is None:
        return pl.BlockSpec(shape, lambda *_: (0,) * nd, pipeline_mode=pl.Buffered(1))
    return pl.BlockSpec((None,) + tuple(shape), lambda *_: (layer,) + (0,) * nd,
                        pipeline_mode=pl.Buffered(1))


def _pick_tile(n, target):
    best = None
    for t in range(LANES, min(n, target) + 1, LANES):
        if n % t == 0:
            best = t
    assert best is not None
    return best


def _rms(x, g):
    ms = jnp.mean(x * x, axis=-1, keepdims=True)
    return x * lax.rsqrt(ms + EPS) * g


def _dot(a, b):
    return jnp.dot(a, b, preferred_element_type=F32)


def _dot_nt(a, b):
    return lax.dot_general(a, b, (((1,), (1,)), ((), ())), preferred_element_type=F32)


def _softplus2(z):
    bits = lax.bitcast_convert_type(z, jnp.uint32) | jnp.uint32(SIGN_BIT)
    neg_abs = lax.bitcast_convert_type(bits, F32)
    return jnp.maximum(z, 0.0) + jnp.log2(1.0 + jnp.exp2(neg_abs))


def _idiv(x, n):
    assert n & (n - 1) == 0, "power-of-two divisor"
    return lax.shift_right_logical(x, n.bit_length() - 1)


def _imod(x, n):
    assert n & (n - 1) == 0, "power-of-two divisor"
    return x & (n - 1)


def _ada_body(cp_ref, cs_ref, w_ref, b_ref, op_ref, os_ref):
    w = w_ref[...].astype(BF16)
    b = b_ref[...]
    for c_ref, o_ref in ((cp_ref, op_ref), (cs_ref, os_ref)):
        c = c_ref[...]
        a = (c * jax.nn.sigmoid(c)).astype(BF16)
        o_ref[...] = _dot(a, w) + b


def _ada(c_prompt, c_sample, w_ada, b_ada):
    depth, d, n = w_ada.shape
    tn = _pick_tile(n, 1024)
    bp, bs = c_prompt.shape[0], c_sample.shape[0]
    return pl.pallas_call(
        _ada_body,
        grid=(depth, n // tn),
        in_specs=[
            pl.BlockSpec((bp, d), lambda l, j: (0, 0)),
            pl.BlockSpec((bs, d), lambda l, j: (0, 0)),
            pl.BlockSpec((None, d, tn), lambda l, j: (l, 0, j)),
            pl.BlockSpec((None, 1, tn), lambda l, j: (l, 0, j)),
        ],
        out_specs=[
            pl.BlockSpec((None, bp, tn), lambda l, j: (l, 0, j)),
            pl.BlockSpec((None, bs, tn), lambda l, j: (l, 0, j)),
        ],
        out_shape=[jax.ShapeDtypeStruct((depth, bp, n), F32),
                   jax.ShapeDtypeStruct((depth, bs, n), F32)],
        compiler_params=_cparams("arbitrary", "arbitrary"),
        name="ada",
    )(c_prompt, c_sample, w_ada, b_ada.reshape(depth, 1, n))


class _Layout:
    def __init__(self, x_shape, prompt, row_tile, seq_tile):
        ng, rt, _ = x_shape
        self.prompt = prompt
        if prompt:
            self.g, self.r = 1, min(row_tile, rt)
            assert rt % self.r == 0
            self.grid = (ng, rt // self.r)
        else:
            self.g, self.r = min(seq_tile, ng), rt
            assert ng % self.g == 0
            self.grid = (ng // self.g, 1)
        self.nt = self.grid[1]

    def act(self, d):
        return pl.BlockSpec((self.g, self.r, d), lambda i, t: (i, t, 0))

    def per_group(self, rows, d, layer=None):
        if layer is None:
            return pl.BlockSpec((self.g, rows, d), lambda i, t: (i, 0, 0))
        return pl.BlockSpec((None, self.g, rows, d), lambda i, t: (layer, i, 0, 0))


def _dwconv(xs, w_ref, b_ref, out, g, r, d, taps):
    off0 = CONV_PAD - (taps - 1)
    gc = min(g, 4)
    rc = min(r, 64)

    def strip_rows(j, carry):
        cs = pl.ds(pl.multiple_of(j * LANES, LANES), LANES)
        bias = b_ref[:, cs]
        for r0 in range(0, r, rc):
            acc = jnp.broadcast_to(bias, (rc, LANES))
            for res in range(SUBLANES):
                z = None
                for k in range(taps):
                    if (k + off0) % SUBLANES != res:
                        continue
                    base = r0 + k + off0 - res
                    term = w_ref[k:k + 1, cs] * xs[0, base:base + rc + SUBLANES, cs]
                    z = term if z is None else z + term
                if z is not None:
                    acc = acc + z[res:res + rc]
            out[0, r0:r0 + rc, cs] = acc
        return carry

    def strip_groups(j, carry):
        cs = pl.ds(pl.multiple_of(j * LANES, LANES), LANES)
        bias = b_ref[:, cs]
        for g0 in range(0, g, gc):
            acc = jnp.broadcast_to(bias, (gc, r, LANES))
            for k in range(taps):
                acc = acc + w_ref[k:k + 1, cs] * xs[g0:g0 + gc, off0 + k:off0 + k + r, cs]
            out[g0:g0 + gc, :, cs] = acc
        return carry

    lax.fori_loop(0, d // LANES, strip_rows if g == 1 else strip_groups, 0)


def _mixer_a_body(lay, d, taps, *refs):
    if lay.prompt:
        (x_ref, mod_ref, gpre, gpost, win, wdw, bdw, lng, lnb, wout,
         o_ref, tail_ref, xs, cv) = refs
        st_ref = None
    else:
        (x_ref, mod_ref, st_ref, gpre, gpost, win, wdw, bdw, lng, lnb, wout,
         o_ref, tail_ref, xs, cv) = refs
    g, r = lay.g, lay.r
    m = g * r
    t = pl.program_id(1)
    if lay.prompt:
        @pl.when(t == 0)
        def _():
            xs[:, 0:CONV_PAD, :] = jnp.zeros((g, CONV_PAD, d), F32)
            xs[:, r + CONV_PAD:, :] = jnp.zeros((g, SUBLANES, d), F32)
    else:
        xs[:, CONV_PAD - (taps - 1):CONV_PAD, :] = st_ref[...]

    x3 = x_ref[...]
    shift, scale, gate = mod_ref[:, 0:1, :], mod_ref[:, 1:2, :], mod_ref[:, 2:3, :]
    h3 = _rms(x3, gpre[...]) * (1.0 + scale) + shift
    ag = _dot(h3.reshape(m, d).astype(BF16), win[...])
    u = ag[:, :d] * jax.nn.sigmoid(ag[:, d:])
    xs[:, CONV_PAD:CONV_PAD + r, :] = u.reshape(g, r, d)
    _dwconv(xs, wdw, bdw, cv, g, r, d, taps)
    c3 = cv[...]
    mu = jnp.mean(c3, axis=-1, keepdims=True)
    dc = c3 - mu
    var = jnp.mean(dc * dc, axis=-1, keepdims=True)
    yn = dc * lax.rsqrt(var + EPS) * lng[...] + lnb[...]
    s = yn * jax.nn.sigmoid(yn)
    y = _dot(s.reshape(m, d).astype(BF16), wout[...])
    o_ref[...] = x3 + gate * _rms(y.reshape(g, r, d), gpost[...])

    lo = r + CONV_PAD - (taps - 1)
    if lay.prompt:
        @pl.when(t == lay.nt - 1)
        def _():
            tail_ref[...] = xs[:, lo:r + CONV_PAD, :]
        xs[:, 0:CONV_PAD, :] = xs[:, r:r + CONV_PAD, :]
    else:
        tail_ref[...] = xs[:, lo:r + CONV_PAD, :]


def _mixer_a(lay, l, x, mod, state, gpre, gpost, win, wdw, bdw, lng, lnb, wout):
    ng, rt, d = x.shape
    taps = wdw.shape[1]
    in_specs = [lay.act(d), lay.per_group(6, d, l)]
    args = [x, mod]
    if not lay.prompt:
        in_specs.append(lay.per_group(taps - 1, d, l))
        args.append(state)
    in_specs += [_const_spec((1, d), l), _const_spec((1, d), l), _const_spec((d, 2 * d), l),
                 _const_spec((taps, d), l), _const_spec((1, d), l), _const_spec((1, d), l),
                 _const_spec((1, d), l), _const_spec((d, d), l)]
    args += [gpre, gpost, win, wdw, bdw, lng, lnb, wout]
    return pl.pallas_call(
        functools.partial(_mixer_a_body, lay, d, taps),
        grid=lay.grid,
        in_specs=in_specs,
        out_specs=[lay.act(d), lay.per_group(taps - 1, d)],
        out_shape=[jax.ShapeDtypeStruct(x.shape, F32),
                   jax.ShapeDtypeStruct((ng, taps - 1, d), F32)],
        scratch_shapes=[pltpu.VMEM((lay.g, lay.r + CONV_PAD + (SUBLANES if lay.prompt else 0), d), F32),
                        pltpu.VMEM((lay.g, lay.r, d), F32)],
        compiler_params=_cparams("arbitrary", "arbitrary"),
        name="mixer_a_p" if lay.prompt else "mixer_a_s",
    )(*args)


def _ffn_body(lay, d, dff, chunks, *refs):
    if lay.prompt:
        (x_ref, mod_ref, gpre, gpost, wup, wdw, bdw, wdown,
         o_ref, tail_ref, gbuf, acc) = refs
        st_ref = None
    else:
        (x_ref, mod_ref, st_ref, gpre, gpost, wup, wdw, bdw, wdown,
         o_ref, tail_ref, gbuf, acc) = refs
    g, r = lay.g, lay.r
    m = g * r
    t = pl.program_id(1)
    if lay.prompt:
        @pl.when(t == 0)
        def _():
            gbuf[:, 0:SUBLANES, :] = jnp.zeros((g, SUBLANES, dff), F32)
    else:
        gbuf[:, SUBLANES - 2:SUBLANES, :] = st_ref[...]

    x3 = x_ref[...]
    shift, scale, gate = mod_ref[:, 3:4, :], mod_ref[:, 4:5, :], mod_ref[:, 5:6, :]
    hb = (_rms(x3, gpre[...]) * (1.0 + scale) + shift).reshape(m, d).astype(BF16)
    for c, (lo, fc) in enumerate(chunks):
        cs = slice(lo, lo + fc)
        gt = _dot(hb, wup[:, lo:lo + fc]).reshape(g, r, fc)
        val = _dot(hb, wup[:, dff + lo:dff + lo + fc]).reshape(g, r, fc)
        gbuf[:, SUBLANES:SUBLANES + r, cs] = gt
        g1 = gbuf[:, SUBLANES - 1:SUBLANES - 1 + r, cs]
        g2 = gbuf[:, SUBLANES - 2:SUBLANES - 2 + r, cs]
        y = wdw[0:1, cs] * g2 + wdw[1:2, cs] * g1 + wdw[2:3, cs] * gt + bdw[:, cs]
        act = 0.5 * y * (1.0 + jnp.tanh(GELU_C * (y + 0.044715 * (y * y * y)))) * val
        contrib = _dot(act.reshape(m, fc).astype(BF16), wdown[cs, :])
        if c == 0:
            acc[...] = contrib
        else:
            acc[...] += contrib
    o_ref[...] = x3 + gate * _rms(acc[...].reshape(g, r, d), gpost[...])

    if lay.prompt:
        @pl.when(t == lay.nt - 1)
        def _():
            tail_ref[...] = gbuf[:, r + SUBLANES - 2:r + SUBLANES, :]
        gbuf[:, 0:SUBLANES, :] = gbuf[:, r:r + SUBLANES, :]
    else:
        tail_ref[...] = gbuf[:, r + SUBLANES - 2:r + SUBLANES, :]


FFN_CHUNK = 768


def _ffn_chunks(dff):
    assert dff % 256 == 0
    return tuple((lo, min(FFN_CHUNK, dff - lo)) for lo in range(0, dff, FFN_CHUNK))


def _ffn(lay, l, x, mod, state, gpre, gpost, wup, wdw, bdw, wdown):
    ng, rt, d = x.shape
    dff = wdown.shape[1]
    chunks = _ffn_chunks(dff)
    in_specs = [lay.act(d), lay.per_group(6, d, l)]
    args = [x, mod]
    if not lay.prompt:
        in_specs.append(lay.per_group(2, dff, l))
        args.append(state)
    in_specs += [_const_spec((1, d), l), _const_spec((1, d), l), _const_spec((d, 2 * dff), l),
                 _const_spec((3, dff), l), _const_spec((1, dff), l), _const_spec((dff, d), l)]
    args += [gpre, gpost, wup, wdw, bdw, wdown]
    return pl.pallas_call(
        functools.partial(_ffn_body, lay, d, dff, chunks),
        grid=lay.grid,
        in_specs=in_specs,
        out_specs=[lay.act(d), lay.per_group(2, dff)],
        out_shape=[jax.ShapeDtypeStruct(x.shape, F32),
                   jax.ShapeDtypeStruct((ng, 2, dff), F32)],
        scratch_shapes=[pltpu.VMEM((lay.g, lay.r + SUBLANES, dff), F32),
                        pltpu.VMEM((lay.g * lay.r, d), F32)],
        compiler_params=_cparams("arbitrary", "arbitrary"),
        name="ffn_p" if lay.prompt else "ffn_s",
    )(*args)


def _kv_body(lay, d, hd, x_ref, g_ref, w_ref, k_ref, v_ref, *bf16_refs):
    m = lay.g * lay.r
    h = _rms(x_ref[...], g_ref[...]).reshape(m, d).astype(BF16)
    kv = _dot(h, w_ref[...])
    k = kv[:, :hd].reshape(lay.g, lay.r, hd)
    v = kv[:, hd:].reshape(lay.g, lay.r, hd)
    k_ref[...] = k
    v_ref[...] = v
    if bf16_refs:
        bf16_refs[0][...] = k.astype(BF16)
        bf16_refs[1][...] = v.astype(BF16)


def _kv_proj(lay, x, g_kv, w_kv):
    ng, rt, d = x.shape
    hd = w_kv.shape[1] // 2
    sd = jax.ShapeDtypeStruct
    out_shape = [sd((ng, rt, hd), F32), sd((ng, rt, hd), F32)]
    if lay.prompt:
        out_shape += [sd((ng, rt, hd), BF16), sd((ng, rt, hd), BF16)]
    return pl.pallas_call(
        functools.partial(_kv_body, lay, d, hd),
        grid=lay.grid,
        in_specs=[lay.act(d), _const_spec((1, d)), _const_spec((d, 2 * hd))],
        out_specs=[lay.act(hd)] * len(out_shape),
        out_shape=out_shape,
        compiler_params=_cparams("arbitrary", "arbitrary"),
        name="kv_p" if lay.prompt else "kv_s",
    )(x, g_kv, w_kv)


def _q_body(lay, d, hd, qscale, x_ref, mod_ref, gpre, w_ref, q_ref):
    m = lay.g * lay.r
    shift, scale = mod_ref[:, 0:1, :], mod_ref[:, 1:2, :]
    h = (_rms(x_ref[...], gpre[...]) * (1.0 + scale) + shift).reshape(m, d).astype(BF16)
    q = _dot(h, w_ref[...]) * qscale
    q_ref[...] = q.reshape(lay.g, lay.r, hd).astype(q_ref.dtype)


def _q_proj(lay, l, j, x, mod, gpre, w_q, qscale):
    ng, rt, d = x.shape
    hd = w_q.shape[2]
    return pl.pallas_call(
        functools.partial(_q_body, lay, d, hd, qscale),
        grid=lay.grid,
        in_specs=[lay.act(d), lay.per_group(6, d, l), _const_spec((1, d), l),
                  _const_spec((d, hd), j)],
        out_specs=lay.act(hd),
        out_shape=jax.ShapeDtypeStruct((ng, rt, hd), BF16 if lay.prompt else F32),
        compiler_params=_cparams("arbitrary", "arbitrary"),
        name="q_p" if lay.prompt else "q_s",
    )(x, mod, gpre, w_q)


def _o_body(lay, d, hd, o_ref, x_ref, mod_ref, gpost, w_ref, out_ref):
    m = lay.g * lay.r
    gate = mod_ref[:, 2:3, :]
    y = _dot(o_ref[...].reshape(m, hd).astype(BF16), w_ref[...])
    out_ref[...] = x_ref[...] + gate * _rms(y.reshape(lay.g, lay.r, d), gpost[...])


def _o_proj(lay, l, j, o, x, mod, gpost, w_o):
    ng, rt, d = x.shape
    hd = w_o.shape[1]
    return pl.pallas_call(
        functools.partial(_o_body, lay, d, hd),
        grid=lay.grid,
        in_specs=[lay.act(hd), lay.act(d), lay.per_group(6, d, l), _const_spec((1, d), l),
                  _const_spec((hd, d), j)],
        out_specs=lay.act(d),
        out_shape=jax.ShapeDtypeStruct(x.shape, F32),
        compiler_params=_cparams("arbitrary", "arbitrary"),
        name="o_p" if lay.prompt else "o_s",
    )(o, x, mod, gpost, w_o)


def _attn_p_body(tq, kb, dh, npair, bias_ref, q_ref, k_ref, v_ref, tri_ref, o_ref, acc):
    grp = pl.program_id(1)
    i = pl.program_id(2)
    nsub = tq // kb
    lane = lax.broadcasted_iota(jnp.int32, (tq, LANES), 1)
    tri = tri_ref[...]
    row = _imod(lax.broadcasted_iota(jnp.int32, (2 * tq, kb), 0), tq)
    col = lax.broadcasted_iota(jnp.int32, (2 * tq, kb), 1)
    qs, bias = [], []
    for p in range(npair):
        qp = q_ref[:, p * LANES:(p + 1) * LANES]
        zero = jnp.zeros_like(qp)
        qs.append(jnp.concatenate([jnp.where(lane < dh, qp, zero), jnp.where(lane >= dh, qp, zero)],
                                  axis=0))
        h0 = (grp * npair + p) * 2
        bias.append((bias_ref[h0] * LOG2E, bias_ref[h0 + 1] * LOG2E))

    def block(p, j, carry, sub):
        ks = pl.ds(pl.multiple_of(j * kb, kb), kb)
        ls = slice(p * LANES, (p + 1) * LANES)
        zz = _dot_nt(qs[p], k_ref[ks, ls])
        z = jnp.concatenate([zz[:tq] + bias[p][0], zz[tq:] + bias[p][1]], axis=0)
        s = _softplus2(z)
        if sub is not None:
            causal = col + sub * kb < row
            s = jnp.where(causal, s, 0.0)
        c = _dot(s, tri)
        e = z - c
        if carry is not None:
            e = e - carry
        att = jnp.exp2(e)
        if sub is not None:
            att = jnp.where(causal, att, 0.0)
        pv = _dot(att, v_ref[ks, ls])
        if carry is None:
            acc[p] = pv
            return c[:, 0:1]
        acc[p] += pv
        return carry + c[:, 0:1]

    carries = [None] * npair
    for sub in range(nsub - 1, -1, -1):
        for p in range(npair):
            carries[p] = block(p, i * nsub + sub, carries[p], sub)

    def step(s, carry):
        carry = list(carry)
        for u in range(nsub):
            for p in range(npair):
                carry[p] = block(p, (i - s) * nsub - 1 - u, carry[p], None)
        return tuple(carry)

    lax.fori_loop(0, i, step, tuple(carries))
    for p in range(npair):
        o_ref[:, p * LANES:(p + 1) * LANES] = jnp.where(
            lane < dh, acc[p, 0:tq, :], acc[p, tq:2 * tq, :]).astype(o_ref.dtype)


def _attn_s_one(ppb, page, r, n_heads, dh, bias_ref, q, kn_rows, vn_rows, tri, k_refs, v_refs):
    n_pages = len(k_refs)
    kb = ppb * page
    hd = n_heads * dh
    hq = n_heads * r
    row_h = _idiv(lax.broadcasted_iota(jnp.int32, (hq, hd), 0), r)
    lane_h = _idiv(lax.broadcasted_iota(jnp.int32, (hq, hd), 1), dh)
    own = row_h == lane_h
    qrep = jnp.broadcast_to(q[None], (n_heads, r, hd)).reshape(hq, hd)
    qbd = jnp.where(own, qrep, 0.0).astype(BF16)
    rh = _idiv(lax.broadcasted_iota(jnp.int32, (hq, kb), 0), r)
    bias = jnp.zeros((hq, kb), F32)
    for h in range(n_heads):
        bias = jnp.where(rh == h, bias_ref[h] * LOG2E, bias)

    qi = _imod(lax.broadcasted_iota(jnp.int32, (hq, LANES), 0), r)
    kj = lax.broadcasted_iota(jnp.int32, (hq, LANES), 1)
    causal = kj < qi
    pad = jnp.zeros((LANES - r, hd), F32)
    kn = jnp.concatenate([kn_rows, pad], axis=0).astype(BF16)
    vn = jnp.concatenate([vn_rows, pad], axis=0).astype(BF16)
    z = _dot_nt(qbd, kn) + bias[:, :LANES]
    s = jnp.where(causal, _softplus2(z), 0.0)
    c = _dot(s, tri[:LANES, :LANES])
    att_new = jnp.where(causal, jnp.exp2(z - c), 0.0)
    carry = c[:, 0:1]
    nb = n_pages // ppb

    def block_t(refs, j):
        pages = [refs[j * ppb + u][...] for u in range(ppb)]
        return jnp.concatenate(pages, axis=1).astype(BF16)

    zs = [_dot(qbd, block_t(k_refs, j)) + bias for j in range(nb)]
    cs = [_dot(_softplus2(zj), tri) for zj in zs]
    acc = _dot(att_new, vn)
    for j in range(nb - 1, -1, -1):
        att = jnp.exp2(zs[j] - cs[j] - carry)
        carry = carry + cs[j][:, 0:1]
        acc = acc + _dot_nt(att, block_t(v_refs, j))
    o = jnp.where(own, acc, 0.0).reshape(n_heads, r, hd)
    return jnp.sum(o, axis=0)


def _attn_body(cfg, pt_ref, bias_ref, q_ref, k_ref, v_ref, tri_ref, qs_ref, kn_ref, vn_ref, *refs):
    del pt_ref
    tq, kb, dh, npair, n_heads, r, spp, n_pages, ppb, page = cfg
    npg = spp * n_pages
    k_refs, v_refs = refs[:npg], refs[npg:2 * npg]
    o_ref, os_ref, acc = refs[2 * npg:]
    _attn_p_body(tq, kb, dh, npair, bias_ref, q_ref, k_ref, v_ref, tri_ref, o_ref, acc)
    tri = tri_ref[...]
    for u in range(spp):
        pages = slice(u * n_pages, (u + 1) * n_pages)
        os_ref[u] = _attn_s_one(ppb, page, r, n_heads, dh, bias_ref, qs_ref[u], kn_ref[u], vn_ref[u],
                                tri, k_refs[pages], v_refs[pages])


def _attention(q_p, kb_p, vb_p, q_s, k_new, v_new, cache_kt, cache_vt, page_table, bias, n_heads):
    b, t, hd = q_p.shape
    ns, r, _ = q_s.shape
    dh = hd // n_heads
    assert 2 * dh == LANES and n_heads % 2 == 0
    npair = 2 if n_heads % 4 == 0 else 1
    kb = min(256, t)
    tq = min(2 * kb, t)
    assert t % tq == 0 and tq % kb == 0
    width = npair * LANES
    n_grp, nq = n_heads // (2 * npair), t // tq
    steps = b * n_grp * nq
    page = cache_kt.shape[2]
    n_pages = page_table.shape[1]
    ppb = max(1, kb // page)
    assert ppb * page == kb and n_pages % ppb == 0 and page % LANES == 0 and r <= LANES
    assert ns % steps == 0, "sample sequences are spread evenly over the prompt grid steps"
    spp = ns // steps
    tri = (jnp.arange(kb)[:, None] >= jnp.arange(kb)[None, :]).astype(BF16)

    def step(bi, g, i):
        return (bi * n_grp + g) * nq + i

    seq = pl.BlockSpec((spp, r, hd), lambda bi, g, i, pt: (step(bi, g, i), 0, 0))

    def page_spec(u, p):
        return pl.BlockSpec((None, hd, page),
                            lambda bi, g, i, pt: (pt[(step(bi, g, i) * spp + u) * n_pages + p], 0, 0))

    pages = [page_spec(u, p) for u in range(spp) for p in range(n_pages)]
    tile = pl.BlockSpec((None, tq, width), lambda bi, g, i, pt: (bi, i, g))
    whole = pl.BlockSpec((None, t, width), lambda bi, g, i, pt: (bi, 0, g))
    grid_spec = pltpu.PrefetchScalarGridSpec(
        num_scalar_prefetch=1,
        grid=(b, n_grp, nq),
        in_specs=[pl.BlockSpec(memory_space=pltpu.SMEM), tile, whole, whole,
                  pl.BlockSpec((kb, kb), lambda bi, g, i, pt: (0, 0)), seq, seq, seq] + pages * 2,
        out_specs=[tile, seq],
        scratch_shapes=[pltpu.VMEM((npair, 2 * tq, LANES), F32)],
    )
    cfg = (tq, kb, dh, npair, n_heads, r, spp, n_pages, ppb, page)
    return pl.pallas_call(
        functools.partial(_attn_body, cfg),
        grid_spec=grid_spec,
        out_shape=[jax.ShapeDtypeStruct((b, t, hd), BF16), jax.ShapeDtypeStruct((ns, r, hd), F32)],
        compiler_params=_cparams("arbitrary", "arbitrary", "arbitrary"),
        name="attn",
    )(page_table.reshape(-1), bias, q_p, kb_p, vb_p, tri, q_s, k_new, v_new,
      *([cache_kt] * (spp * n_pages)), *([cache_vt] * (spp * n_pages)))


def kernel(x_prompt, x_sample, c_prompt, c_sample, state_conv_a, state_ffn_conv, cache_k, cache_v, page_table, w_ada, b_ada, g_pre_mix, g_post_mix, g_pre_ffn, g_post_ffn, w_a_in, w_a_dw, b_a_dw, ln_a_g, ln_a_b, w_a_out, g_kv, w_kv, w_q, w_o, b_sb, w_ffn_up, w_ffn_dw, b_ffn_dw, w_ffn_down):
    depth, d = g_pre_mix.shape
    n_a = w_a_in.shape[0]
    n_heads, dh = cache_k.shape[2], cache_k.shape[3]
    hd = n_heads * dh
    dff = w_ffn_down.shape[1]

    def vec(a):
        return a.reshape(a.shape[0], 1, a.shape[1])

    g_pre_mix, g_post_mix, g_pre_ffn, g_post_ffn = map(vec, (g_pre_mix, g_post_mix, g_pre_ffn, g_post_ffn))
    b_a_dw, ln_a_g, ln_a_b, b_ffn_dw = map(vec, (b_a_dw, ln_a_g, ln_a_b, b_ffn_dw))
    w_a_in, w_a_out, w_kv, w_q, w_o, w_ffn_up, w_ffn_down = (
        w.astype(BF16) for w in (w_a_in, w_a_out, w_kv, w_q, w_o, w_ffn_up, w_ffn_down))
    g_kv = g_kv.reshape(1, d)

    mod_p, mod_s = _ada(c_prompt, c_sample, w_ada, b_ada)
    n_phys, page = cache_k.shape[:2]
    past_kv = tuple(c.transpose(0, 2, 3, 1).reshape(n_phys, hd, page) for c in (cache_k, cache_v))
    groups = [
        dict(x=x_prompt, mod=mod_p.reshape(depth, -1, 6, d), conv_state=None, ffn_state=None,
             lay_a=_Layout(x_prompt.shape, True, 512, 0), lay_f=_Layout(x_prompt.shape, True, 512, 0),
             lay_d=_Layout(x_prompt.shape, True, 512, 0)),
        dict(x=x_sample, mod=mod_s.reshape(depth, -1, 6, d), conv_state=state_conv_a,
             ffn_state=state_ffn_conv,
             lay_a=_Layout(x_sample.shape, False, 0, 32), lay_f=_Layout(x_sample.shape, False, 0, 32),
             lay_d=_Layout(x_sample.shape, False, 0, 64)),
    ]
    for grp in groups:
        grp["conv_tails"], grp["ffn_tails"] = [], []
    for l in range(depth):
        if l < n_a:
            for grp in groups:
                grp["x"], tail = _mixer_a(grp["lay_a"], l, grp["x"], grp["mod"], grp["conv_state"],
                                          g_pre_mix, g_post_mix, w_a_in, w_a_dw, b_a_dw, ln_a_g, ln_a_b,
                                          w_a_out)
                grp["conv_tails"].append(tail)
        else:
            j = l - n_a
            q_p, q_s = (_q_proj(grp["lay_d"], l, j, grp["x"], grp["mod"], g_pre_mix, w_q,
                                dh ** -0.5 * LOG2E) for grp in groups)
            (_, _, kb_p, vb_p), (k_new, v_new) = groups[0]["kv"], groups[1]["kv"]
            outs = _attention(q_p, kb_p, vb_p, q_s, k_new, v_new, *past_kv, page_table, b_sb[j], n_heads)
            for grp, o in zip(groups, outs):
                grp["x"] = _o_proj(grp["lay_d"], l, j, o, grp["x"], grp["mod"], g_post_mix, w_o)
        for grp in groups:
            grp["x"], tail = _ffn(grp["lay_f"], l, grp["x"], grp["mod"], grp["ffn_state"], g_pre_ffn,
                                  g_post_ffn, w_ffn_up, w_ffn_dw, b_ffn_dw, w_ffn_down)
            grp["ffn_tails"].append(tail)
            if l == n_a - 1:
                grp["kv"] = _kv_proj(grp["lay_d"], grp["x"], g_kv, w_kv)
    results = []
    for grp in groups:
        x = grp["x"]
        shp = x.shape[:2] + (n_heads, dh)
        results.append((x, jnp.stack(grp["conv_tails"]), jnp.stack(grp["ffn_tails"]),
                        grp["kv"][0].reshape(shp), grp["kv"][1].reshape(shp)))
    (yp, cap, ffp, kp, vp), (ys, cas, ffs, ks, vs) = results
    return (yp, ys, cap, cas, ffp, ffs, kp, vp, ks, vs)
```

```python
import functools

import jax
import jax.numpy as jnp
from jax import lax
from jax.experimental import pallas as pl
from jax.experimental.pallas import tpu as pltpu

F32 = jnp.float32
BF16 = jnp.bfloat16
EPS = 1e-6
LANES = 128
SUBLANES = 8
MXU_WIDTH = 256
CONV_PAD = 32
VMEM_LIMIT = 56 * 1024 * 1024
GELU_C = 0.7978845608028654
LOG2E = 1.4426950408889634
SIGN_BIT = 0x80000000


def _cparams(*sem):
    return pltpu.CompilerParams(dimension_semantics=sem, vmem_limit_bytes=VMEM_LIMIT)


def _const_spec(shape, layer=None):
    nd = len(shape)
    if layer is None:
        return pl.BlockSpec(shape, lambda *_: (0,) * nd, pipeline_mode=pl.Buffered(1))
    return pl.BlockSpec((None,) + tuple(shape), lambda *_: (layer,) + (0,) * nd,
                        pipeline_mode=pl.Buffered(1))


def _pick_tile(n, target):
    best = None
    for t in range(LANES, min(n, target) + 1, LANES):
        if n % t == 0:
            best = t
    assert best is not None
    return best


def _rms(x, g):
    ms = jnp.mean(x * x, axis=-1, keepdims=True)
    return x * lax.rsqrt(ms + EPS) * g


def _dot(a, b):
    return jnp.dot(a, b, preferred_element_type=F32)


def _dot_nt(a, b):
    return lax.dot_general(a, b, (((1,), (1,)), ((), ())), preferred_element_type=F32)


def _softplus2(z):
    bits = lax.bitcast_convert_type(z, jnp.uint32) | jnp.uint32(SIGN_BIT)
    neg_abs = lax.bitcast_convert_type(bits, F32)
    return jnp.maximum(z, 0.0) + jnp.log2(1.0 + jnp.exp2(neg_abs))


def _idiv(x, n):
    assert n & (n - 1) == 0, "power-of-two divisor"
    return lax.shift_right_logical(x, n.bit_length() - 1)


def _imod(x, n):
    assert n & (n - 1) == 0, "power-of-two divisor"
    return x & (n - 1)


def _ada_body(cp_ref, cs_ref, w_ref, b_ref, op_ref, os_ref):
    w = w_ref[...].astype(BF16)
    b = b_ref[...]
    for c_ref, o_ref in ((cp_ref, op_ref), (cs_ref, os_ref)):
        c = c_ref[...]
        a = (c * jax.nn.sigmoid(c)).astype(BF16)
        o_ref[...] = _dot(a, w) + b


def _ada(c_prompt, c_sample, w_ada, b_ada):
    depth, d, n = w_ada.shape
    tn = _pick_tile(n, 1024)
    bp, bs = c_prompt.shape[0], c_sample.shape[0]
    return pl.pallas_call(
        _ada_body,
        grid=(depth, n // tn),
        in_specs=[
            pl.BlockSpec((bp, d), lambda l, j: (0, 0)),
            pl.BlockSpec((bs, d), lambda l, j: (0, 0)),
            pl.BlockSpec((None, d, tn), lambda l, j: (l, 0, j)),
            pl.BlockSpec((None, 1, tn), lambda l, j: (l, 0, j)),
        ],
        out_specs=[
            pl.BlockSpec((None, bp, tn), lambda l, j: (l, 0, j)),
            pl.BlockSpec((None, bs, tn), lambda l, j: (l, 0, j)),
        ],
        out_shape=[jax.ShapeDtypeStruct((depth, bp, n), F32),
                   jax.ShapeDtypeStruct((depth, bs, n), F32)],
        compiler_params=_cparams("arbitrary", "arbitrary"),
        name="ada",
    )(c_prompt, c_sample, w_ada, b_ada.reshape(depth, 1, n))


class _Layout:
    def __init__(self, x_shape, prompt, row_tile, seq_tile):
        ng, rt, _ = x_shape
        self.prompt = prompt
        if prompt:
            self.g, self.r = 1, min(row_tile, rt)
            assert rt % self.r == 0
            self.grid = (ng, rt // self.r)
        else:
            self.g, self.r = min(seq_tile, ng), rt
            assert ng % self.g == 0
            self.grid = (ng // self.g, 1)
        self.nt = self.grid[1]

    def act(self, d):
        return pl.BlockSpec((self.g, self.r, d), lambda i, t: (i, t, 0))

    def per_group(self, rows, d, layer=None):
        if layer is None:
            return pl.BlockSpec((self.g, rows, d), lambda i, t: (i, 0, 0))
        return pl.BlockSpec((None, self.g, rows, d), lambda i, t: (layer, i, 0, 0))


def _dwconv_rows(xs, w_ref, b_ref, out, r, taps, cs):
    off0 = CONV_PAD - (taps - 1)
    rc = min(r, 64)
    bias = b_ref[:, cs]
    for r0 in range(0, r, rc):
        acc = jnp.broadcast_to(bias, (rc, LANES))
        for res in range(SUBLANES):
            z = None
            for k in range(taps):
                if (k + off0) % SUBLANES != res:
                    continue
                base = r0 + k + off0 - res
                term = w_ref[k:k + 1, cs] * xs[0, base:base + rc + SUBLANES, cs]
                z = term if z is None else z + term
            if z is not None:
                acc = acc + z[res:res + rc]
        out[0, r0:r0 + rc, cs] = acc


def _dwconv_groups(xs, w_ref, b_ref, out, g, r, d, taps):
    off0 = CONV_PAD - (taps - 1)
    gc = min(g, 4)

    def strip(j, carry):
        cs = pl.ds(pl.multiple_of(j * LANES, LANES), LANES)
        bias = b_ref[:, cs]
        for g0 in range(0, g, gc):
            acc = jnp.broadcast_to(bias, (gc, r, LANES))
            for k in range(taps):
                acc = acc + w_ref[k:k + 1, cs] * xs[g0:g0 + gc, off0 + k:off0 + k + r, cs]
            out[g0:g0 + gc, :, cs] = acc
        return carry

    lax.fori_loop(0, d // LANES, strip, 0)


def _mixer_a_body(lay, d, taps, *refs):
    if lay.prompt:
        (x_ref, mod_ref, gpre, gpost, win, wdw, bdw, lng, lnb, wout,
         o_ref, tail_ref, xs, cv) = refs
        st_ref = None
    else:
        (x_ref, mod_ref, st_ref, gpre, gpost, win, wdw, bdw, lng, lnb, wout,
         o_ref, tail_ref, xs, cv) = refs
    g, r = lay.g, lay.r
    m = g * r
    t = pl.program_id(1)
    if lay.prompt:
        @pl.when(t == 0)
        def _():
            xs[:, 0:CONV_PAD, :] = jnp.zeros((g, CONV_PAD, d), F32)
            xs[:, r + CONV_PAD:, :] = jnp.zeros((g, SUBLANES, d), F32)
    else:
        xs[:, CONV_PAD - (taps - 1):CONV_PAD, :] = st_ref[...]

    x3 = x_ref[...]
    shift, scale, gate = mod_ref[:, 0:1, :], mod_ref[:, 1:2, :], mod_ref[:, 2:3, :]
    h3 = _rms(x3, gpre[...]) * (1.0 + scale) + shift
    hb = h3.reshape(m, d).astype(BF16)
    if lay.prompt:
        for j0 in range(0, d, MXU_WIDTH):
            a = _dot(hb, win[:, j0:j0 + MXU_WIDTH])
            gt = _dot(hb, win[:, d + j0:d + j0 + MXU_WIDTH])
            xs[:, CONV_PAD:CONV_PAD + r, j0:j0 + MXU_WIDTH] = (a * jax.nn.sigmoid(gt)).reshape(g, r, -1)
            for c0 in range(j0, j0 + MXU_WIDTH, LANES):
                _dwconv_rows(xs, wdw, bdw, cv, r, taps, slice(c0, c0 + LANES))
    else:
        ag = _dot(hb, win[...])
        u = ag[:, :d] * jax.nn.sigmoid(ag[:, d:])
        xs[:, CONV_PAD:CONV_PAD + r, :] = u.reshape(g, r, d)
        _dwconv_groups(xs, wdw, bdw, cv, g, r, d, taps)
    c3 = cv[...]
    mu = jnp.mean(c3, axis=-1, keepdims=True)
    dc = c3 - mu
    var = jnp.mean(dc * dc, axis=-1, keepdims=True)
    yn = dc * lax.rsqrt(var + EPS) * lng[...] + lnb[...]
    s = yn * jax.nn.sigmoid(yn)
    y = _dot(s.reshape(m, d).astype(BF16), wout[...])
    o_ref[...] = x3 + gate * _rms(y.reshape(g, r, d), gpost[...])

    lo = r + CONV_PAD - (taps - 1)
    if lay.prompt:
        @pl.when(t == lay.nt - 1)
        def _():
            tail_ref[...] = xs[:, lo:r + CONV_PAD, :]
        xs[:, 0:CONV_PAD, :] = xs[:, r:r + CONV_PAD, :]
    else:
        tail_ref[...] = xs[:, lo:r + CONV_PAD, :]


def _mixer_a(lay, l, x, mod, state, gpre, gpost, win, wdw, bdw, lng, lnb, wout):
    ng, rt, d = x.shape
    taps = wdw.shape[1]
    in_specs = [lay.act(d), lay.per_group(6, d, l)]
    args = [x, mod]
    if not lay.prompt:
        in_specs.append(lay.per_group(taps - 1, d, l))
        args.append(state)
    in_specs += [_const_spec((1, d), l), _const_spec((1, d), l), _const_spec((d, 2 * d), l),
                 _const_spec((taps, d), l), _const_spec((1, d), l), _const_spec((1, d), l),
                 _const_spec((1, d), l), _const_spec((d, d), l)]
    args += [gpre, gpost, win, wdw, bdw, lng, lnb, wout]
    return pl.pallas_call(
        functools.partial(_mixer_a_body, lay, d, taps),
        grid=lay.grid,
        in_specs=in_specs,
        out_specs=[lay.act(d), lay.per_group(taps - 1, d)],
        out_shape=[jax.ShapeDtypeStruct(x.shape, F32),
                   jax.ShapeDtypeStruct((ng, taps - 1, d), F32)],
        scratch_shapes=[pltpu.VMEM((lay.g, lay.r + CONV_PAD + (SUBLANES if lay.prompt else 0), d), F32),
                        pltpu.VMEM((lay.g, lay.r, d), F32)],
        compiler_params=_cparams("arbitrary", "arbitrary"),
        name="mixer_a_p" if lay.prompt else "mixer_a_s",
    )(*args)


def _ffn_body(lay, d, dff, chunks, *refs):
    if lay.prompt:
        (x_ref, mod_ref, gpre, gpost, wup, wdw, bdw, wdown,
         o_ref, tail_ref, gbuf, acc) = refs
        st_ref = None
    else:
        (x_ref, mod_ref, st_ref, gpre, gpost, wup, wdw, bdw, wdown,
         o_ref, tail_ref, gbuf, acc) = refs
    g, r = lay.g, lay.r
    m = g * r
    t = pl.program_id(1)
    if lay.prompt:
        @pl.when(t == 0)
        def _():
            gbuf[:, 0:SUBLANES, :] = jnp.zeros((g, SUBLANES, dff), F32)
    else:
        gbuf[:, SUBLANES - 2:SUBLANES, :] = st_ref[...]

    x3 = x_ref[...]
    shift, scale, gate = mod_ref[:, 3:4, :], mod_ref[:, 4:5, :], mod_ref[:, 5:6, :]
    hb = (_rms(x3, gpre[...]) * (1.0 + scale) + shift).reshape(m, d).astype(BF16)
    for c, (lo, fc) in enumerate(chunks):
        cs = slice(lo, lo + fc)
        gt = _dot(hb, wup[:, lo:lo + fc]).reshape(g, r, fc)
        val = _dot(hb, wup[:, dff + lo:dff + lo + fc]).reshape(g, r, fc)
        gbuf[:, SUBLANES:SUBLANES + r, cs] = gt
        g1 = gbuf[:, SUBLANES - 1:SUBLANES - 1 + r, cs]
        g2 = gbuf[:, SUBLANES - 2:SUBLANES - 2 + r, cs]
        y = wdw[0:1, cs] * g2 + wdw[1:2, cs] * g1 + wdw[2:3, cs] * gt + bdw[:, cs]
        act = 0.5 * y * (1.0 + jnp.tanh(GELU_C * (y + 0.044715 * (y * y * y)))) * val
        contrib = _dot(act.reshape(m, fc).astype(BF16), wdown[cs, :])
        if c == 0:
            acc[...] = contrib
        else:
            acc[...] += contrib
    o_ref[...] = x3 + gate * _rms(acc[...].reshape(g, r, d), gpost[...])

    if lay.prompt:
        @pl.when(t == lay.nt - 1)
        def _():
            tail_ref[...] = gbuf[:, r + SUBLANES - 2:r + SUBLANES, :]
        gbuf[:, 0:SUBLANES, :] = gbuf[:, r:r + SUBLANES, :]
    else:
        tail_ref[...] = gbuf[:, r + SUBLANES - 2:r + SUBLANES, :]


FFN_CHUNK = 768

def _ffn_chunks(dff):
    assert dff % 256 == 0
    return tuple((lo, min(FFN_CHUNK, dff - lo)) for lo in range(0, dff, FFN_CHUNK))


def _ffn(lay, l, x, mod, state, gpre, gpost, wup, wdw, bdw, wdown):
    ng, rt, d = x.shape
    dff = wdown.shape[1]
    chunks = _ffn_chunks(dff)
    in_specs = [lay.act(d), lay.per_group(6, d, l)]
    args = [x, mod]
    if not lay.prompt:
        in_specs.append(lay.per_group(2, dff, l))
        args.append(state)
    in_specs += [_const_spec((1, d), l), _const_spec((1, d), l), _const_spec((d, 2 * dff), l),
                 _const_spec((3, dff), l), _const_spec((1, dff), l), _const_spec((dff, d), l)]
    args += [gpre, gpost, wup, wdw, bdw, wdown]
    return pl.pallas_call(
        functools.partial(_ffn_body, lay, d, dff, chunks),
        grid=lay.grid,
        in_specs=in_specs,
        out_specs=[lay.act(d), lay.per_group(2, dff)],
        out_shape=[jax.ShapeDtypeStruct(x.shape, F32),
                   jax.ShapeDtypeStruct((ng, 2, dff), F32)],
        scratch_shapes=[pltpu.VMEM((lay.g, lay.r + SUBLANES, dff), F32),
                        pltpu.VMEM((lay.g * lay.r, d), F32)],
        compiler_params=_cparams("arbitrary", "arbitrary"),
        name="ffn_p" if lay.prompt else "ffn_s",
    )(*args)


def _kv_body(lay, d, hd, x_ref, g_ref, w_ref, k_ref, v_ref, *bf16_refs):
    m = lay.g * lay.r
    h = _rms(x_ref[...], g_ref[...]).reshape(m, d).astype(BF16)
    kv = _dot(h, w_ref[...])
    k = kv[:, :hd].reshape(lay.g, lay.r, hd)
    v = kv[:, hd:].reshape(lay.g, lay.r, hd)
    k_ref[...] = k
    v_ref[...] = v
    if bf16_refs:
        bf16_refs[0][...] = k.astype(BF16)
        bf16_refs[1][...] = v.astype(BF16)


def _kv_proj(lay, x, g_kv, w_kv):
    ng, rt, d = x.shape
    hd = w_kv.shape[1] // 2
    sd = jax.ShapeDtypeStruct
    out_shape = [sd((ng, rt, hd), F32), sd((ng, rt, hd), F32)]
    if lay.prompt:
        out_shape += [sd((ng, rt, hd), BF16), sd((ng, rt, hd), BF16)]
    return pl.pallas_call(
        functools.partial(_kv_body, lay, d, hd),
        grid=lay.grid,
        in_specs=[lay.act(d), _const_spec((1, d)), _const_spec((d, 2 * hd))],
        out_specs=[lay.act(hd)] * len(out_shape),
        out_shape=out_shape,
        compiler_params=_cparams("arbitrary", "arbitrary"),
        name="kv_p" if lay.prompt else "kv_s",
    )(x, g_kv, w_kv)


def _q_body(lay, d, hd, qscale, x_ref, mod_ref, gpre, w_ref, q_ref):
    m = lay.g * lay.r
    shift, scale = mod_ref[:, 0:1, :], mod_ref[:, 1:2, :]
    h = (_rms(x_ref[...], gpre[...]) * (1.0 + scale) + shift).reshape(m, d).astype(BF16)
    q = _dot(h, w_ref[...]) * qscale
    q_ref[...] = q.reshape(lay.g, lay.r, hd).astype(q_ref.dtype)


def _q_proj(lay, l, j, x, mod, gpre, w_q, qscale):
    ng, rt, d = x.shape
    hd = w_q.shape[2]
    return pl.pallas_call(
        functools.partial(_q_body, lay, d, hd, qscale),
        grid=lay.grid,
        in_specs=[lay.act(d), lay.per_group(6, d, l), _const_spec((1, d), l),
                  _const_spec((d, hd), j)],
        out_specs=lay.act(hd),
        out_shape=jax.ShapeDtypeStruct((ng, rt, hd), BF16 if lay.prompt else F32),
        compiler_params=_cparams("arbitrary", "arbitrary"),
        name="q_p" if lay.prompt else "q_s",
    )(x, mod, gpre, w_q)


def _o_body(lay, d, hd, o_ref, x_ref, mod_ref, gpost, w_ref, out_ref):
    m = lay.g * lay.r
    gate = mod_ref[:, 2:3, :]
    y = _dot(o_ref[...].reshape(m, hd).astype(BF16), w_ref[...])
    out_ref[...] = x_ref[...] + gate * _rms(y.reshape(lay.g, lay.r, d), gpost[...])


def _o_proj(lay, l, j, o, x, mod, gpost, w_o):
    ng, rt, d = x.shape
    hd = w_o.shape[1]
    return pl.pallas_call(
        functools.partial(_o_body, lay, d, hd),
        grid=lay.grid,
        in_specs=[lay.act(hd), lay.act(d), lay.per_group(6, d, l), _const_spec((1, d), l),
                  _const_spec((hd, d), j)],
        out_specs=lay.act(d),
        out_shape=jax.ShapeDtypeStruct(x.shape, F32),
        compiler_params=_cparams("arbitrary", "arbitrary"),
        name="o_p" if lay.prompt else "o_s",
    )(o, x, mod, gpost, w_o)


def _attn_p_body(tq, kb, dh, npair, bias_ref, q_ref, k_ref, v_ref, tri_ref, o_ref, acc):
    grp = pl.program_id(1)
    i = pl.program_id(2)
    nsub = tq // kb
    lane = lax.broadcasted_iota(jnp.int32, (tq, LANES), 1)
    tri = tri_ref[...]
    row = _imod(lax.broadcasted_iota(jnp.int32, (2 * tq, kb), 0), tq)
    col = lax.broadcasted_iota(jnp.int32, (2 * tq, kb), 1)
    qs, bias = [], []
    for p in range(npair):
        qp = q_ref[:, p * LANES:(p + 1) * LANES]
        zero = jnp.zeros_like(qp)
        qs.append(jnp.concatenate([jnp.where(lane < dh, qp, zero), jnp.where(lane >= dh, qp, zero)],
                                  axis=0))
        h0 = (grp * npair + p) * 2
        bias.append((bias_ref[h0] * LOG2E, bias_ref[h0 + 1] * LOG2E))

    def block(p, j, carry, sub):
        ks = pl.ds(pl.multiple_of(j * kb, kb), kb)
        ls = slice(p * LANES, (p + 1) * LANES)
        zz = _dot_nt(qs[p], k_ref[ks, ls])
        z = jnp.concatenate([zz[:tq] + bias[p][0], zz[tq:] + bias[p][1]], axis=0)
        s = _softplus2(z)
        if sub is not None:
            causal = col + sub * kb < row
            s = jnp.where(causal, s, 0.0)
        c = _dot(s, tri)
        att = jnp.exp2(z - c - carry)
        if sub is not None:
            att = jnp.where(causal, att, 0.0)
        acc[p] += _dot(att, v_ref[ks, ls])
        return carry + c[:, 0:1]

    def first_block(p):
        r0 = tq - kb
        ks = pl.ds(pl.multiple_of((i * nsub + nsub - 1) * kb, kb), kb)
        ls = slice(p * LANES, (p + 1) * LANES)
        q2 = jnp.concatenate([qs[p][r0:tq], qs[p][tq + r0:2 * tq]], axis=0)
        zz = _dot_nt(q2, k_ref[ks, ls])
        z = jnp.concatenate([zz[:kb] + bias[p][0], zz[kb:] + bias[p][1]], axis=0)
        causal = (lax.broadcasted_iota(jnp.int32, (2 * kb, kb), 1)
                  < _imod(lax.broadcasted_iota(jnp.int32, (2 * kb, kb), 0), kb))
        c = _dot(jnp.where(causal, _softplus2(z), 0.0), tri)
        att = jnp.where(causal, jnp.exp2(z - c), 0.0)
        pv = _dot(att, v_ref[ks, ls])
        if r0 == 0:
            acc[p] = pv
            return c[:, 0:1]
        zero = jnp.zeros((r0, LANES), F32)
        acc[p] = jnp.concatenate([zero, pv[:kb], zero, pv[kb:]], axis=0)
        c0 = c[:, 0:LANES]
        return jnp.concatenate([zero, c0[:kb], zero, c0[kb:]], axis=0)[:, 0:1]

    carries = [first_block(p) for p in range(npair)]
    for sub in range(nsub - 2, -1, -1):
        for p in range(npair):
            carries[p] = block(p, i * nsub + sub, carries[p], sub)

    def step(s, carry):
        carry = list(carry)
        for u in range(nsub):
            for p in range(npair):
                carry[p] = block(p, (i - s) * nsub - 1 - u, carry[p], None)
        return tuple(carry)

    lax.fori_loop(0, i, step, tuple(carries))
    for p in range(npair):
        o_ref[:, p * LANES:(p + 1) * LANES] = jnp.where(
            lane < dh, acc[p, 0:tq, :], acc[p, tq:2 * tq, :]).astype(o_ref.dtype)


def _attn_s_one(ppb, page, r, n_heads, dh, bias_ref, q, kn_rows, vn_rows, tri, k_refs, v_refs):
    n_pages = len(k_refs)
    kb = ppb * page
    hd = n_heads * dh
    hq = n_heads * r
    row_h = _idiv(lax.broadcasted_iota(jnp.int32, (hq, hd), 0), r)
    lane_h = _idiv(lax.broadcasted_iota(jnp.int32, (hq, hd), 1), dh)
    own = row_h == lane_h
    qrep = jnp.broadcast_to(q[None], (n_heads, r, hd)).reshape(hq, hd)
    qbd = jnp.where(own, qrep, 0.0).astype(BF16)
    rh = _idiv(lax.broadcasted_iota(jnp.int32, (hq, kb), 0), r)
    bias = jnp.zeros((hq, kb), F32)
    for h in range(n_heads):
        bias = jnp.where(rh == h, bias_ref[h] * LOG2E, bias)

    qi = _imod(lax.broadcasted_iota(jnp.int32, (hq, LANES), 0), r)
    kj = lax.broadcasted_iota(jnp.int32, (hq, LANES), 1)
    causal = kj < qi
    pad = jnp.zeros((LANES - r, hd), F32)
    kn = jnp.concatenate([kn_rows, pad], axis=0).astype(BF16)
    vn = jnp.concatenate([vn_rows, pad], axis=0).astype(BF16)
    z = _dot_nt(qbd, kn) + bias[:, :LANES]
    s = jnp.where(causal, _softplus2(z), 0.0)
    c = _dot(s, tri[:LANES, :LANES])
    att_new = jnp.where(causal, jnp.exp2(z - c), 0.0)
    carry = c[:, 0:1]
    nb = n_pages // ppb

    def block_t(refs, j):
        pages = [refs[j * ppb + u][...] for u in range(ppb)]
        return jnp.concatenate(pages, axis=1).astype(BF16)

    zs = [_dot(qbd, block_t(k_refs, j)) + bias for j in range(nb)]
    cs = [_dot(_softplus2(zj), tri) for zj in zs]
    acc = _dot(att_new, vn)
    for j in range(nb - 1, -1, -1):
        att = jnp.exp2(zs[j] - cs[j] - carry)
        carry = carry + cs[j][:, 0:1]
        acc = acc + _dot_nt(att, block_t(v_refs, j))
    o = jnp.where(own, acc, 0.0).reshape(n_heads, r, hd)
    return jnp.sum(o, axis=0)


def _attn_body(cfg, pt_ref, bias_ref, q_ref, k_ref, v_ref, tri_ref, qs_ref, kn_ref, vn_ref, *refs):
    del pt_ref
    tq, kb, dh, npair, n_heads, r, spp, n_pages, ppb, page = cfg
    npg = spp * n_pages
    k_refs, v_refs = refs[:npg], refs[npg:2 * npg]
    o_ref, os_ref, acc = refs[2 * npg:]
    _attn_p_body(tq, kb, dh, npair, bias_ref, q_ref, k_ref, v_ref, tri_ref, o_ref, acc)
    tri = tri_ref[...]
    for u in range(spp):
        pages = slice(u * n_pages, (u + 1) * n_pages)
        os_ref[u] = _attn_s_one(ppb, page, r, n_heads, dh, bias_ref, qs_ref[u], kn_ref[u], vn_ref[u],
                                tri, k_refs[pages], v_refs[pages])


def _attention(q_p, kb_p, vb_p, q_s, k_new, v_new, cache_kt, cache_vt, page_table, bias, n_heads):
    b, t, hd = q_p.shape
    ns, r, _ = q_s.shape
    dh = hd // n_heads
    assert 2 * dh == LANES and n_heads % 2 == 0
    npair = 2 if n_heads % 4 == 0 else 1
    kb = min(256, t)
    tq = min(2 * kb, t)
    assert t % tq == 0 and tq % kb == 0
    width = npair * LANES
    n_grp, nq = n_heads // (2 * npair), t // tq
    steps = b * n_grp * nq
    page = cache_kt.shape[2]
    n_pages = page_table.shape[1]
    ppb = max(1, kb // page)
    assert ppb * page == kb and n_pages % ppb == 0 and page % LANES == 0 and r <= LANES
    assert ns % steps == 0, "sample sequences are spread evenly over the prompt grid steps"
    spp = ns // steps
    tri = (jnp.arange(kb)[:, None] >= jnp.arange(kb)[None, :]).astype(BF16)

    def step(bi, g, i):
        return (bi * n_grp + g) * nq + i

    seq = pl.BlockSpec((spp, r, hd), lambda bi, g, i, pt: (step(bi, g, i), 0, 0))

    def page_spec(u, p):
        return pl.BlockSpec((None, hd, page),
                            lambda bi, g, i, pt: (pt[(step(bi, g, i) * spp + u) * n_pages + p], 0, 0))

    pages = [page_spec(u, p) for u in range(spp) for p in range(n_pages)]
    tile = pl.BlockSpec((None, tq, width), lambda bi, g, i, pt: (bi, i, g))
    whole = pl.BlockSpec((None, t, width), lambda bi, g, i, pt: (bi, 0, g))
    grid_spec = pltpu.PrefetchScalarGridSpec(
        num_scalar_prefetch=1,
        grid=(b, n_grp, nq),
        in_specs=[pl.BlockSpec(memory_space=pltpu.SMEM), tile, whole, whole,
                  pl.BlockSpec((kb, kb), lambda bi, g, i, pt: (0, 0)), seq, seq, seq] + pages * 2,
        out_specs=[tile, seq],
        scratch_shapes=[pltpu.VMEM((npair, 2 * tq, LANES), F32)],
    )
    cfg = (tq, kb, dh, npair, n_heads, r, spp, n_pages, ppb, page)
    return pl.pallas_call(
        functools.partial(_attn_body, cfg),
        grid_spec=grid_spec,
        out_shape=[jax.ShapeDtypeStruct((b, t, hd), BF16), jax.ShapeDtypeStruct((ns, r, hd), F32)],
        compiler_params=_cparams("arbitrary", "arbitrary", "arbitrary"),
        name="attn",
    )(page_table.reshape(-1), bias, q_p, kb_p, vb_p, tri, q_s, k_new, v_new,
      *([cache_kt] * (spp * n_pages)), *([cache_vt] * (spp * n_pages)))


def kernel(x_prompt, x_sample, c_prompt, c_sample, state_conv_a, state_ffn_conv, cache_k, cache_v, page_table, w_ada, b_ada, g_pre_mix, g_post_mix, g_pre_ffn, g_post_ffn, w_a_in, w_a_dw, b_a_dw, ln_a_g, ln_a_b, w_a_out, g_kv, w_kv, w_q, w_o, b_sb, w_ffn_up, w_ffn_dw, b_ffn_dw, w_ffn_down):
    depth, d = g_pre_mix.shape
    n_a = w_a_in.shape[0]
    n_heads, dh = cache_k.shape[2], cache_k.shape[3]
    hd = n_heads * dh
    dff = w_ffn_down.shape[1]

    def vec(a):
        return a.reshape(a.shape[0], 1, a.shape[1])

    g_pre_mix, g_post_mix, g_pre_ffn, g_post_ffn = map(vec, (g_pre_mix, g_post_mix, g_pre_ffn, g_post_ffn))
    b_a_dw, ln_a_g, ln_a_b, b_ffn_dw = map(vec, (b_a_dw, ln_a_g, ln_a_b, b_ffn_dw))
    w_a_in, w_a_out, w_kv, w_q, w_o, w_ffn_up, w_ffn_down = (
        w.astype(BF16) for w in (w_a_in, w_a_out, w_kv, w_q, w_o, w_ffn_up, w_ffn_down))
    g_kv = g_kv.reshape(1, d)

    mod_p, mod_s = _ada(c_prompt, c_sample, w_ada, b_ada)
    n_phys, page = cache_k.shape[:2]
    past_kv = tuple(c.transpose(0, 2, 3, 1).reshape(n_phys, hd, page) for c in (cache_k, cache_v))
    groups = [
        dict(x=x_prompt, mod=mod_p.reshape(depth, -1, 6, d), conv_state=None, ffn_state=None,
             lay_a=_Layout(x_prompt.shape, True, 512, 0), lay_f=_Layout(x_prompt.shape, True, 512, 0),
             lay_d=_Layout(x_prompt.shape, True, 512, 0)),
        dict(x=x_sample, mod=mod_s.reshape(depth, -1, 6, d), conv_state=state_conv_a,
             ffn_state=state_ffn_conv,
             lay_a=_Layout(x_sample.shape, False, 0, 32), lay_f=_Layout(x_sample.shape, False, 0, 32),
             lay_d=_Layout(x_sample.shape, False, 0, 64)),
    ]
    for grp in groups:
        grp["conv_tails"], grp["ffn_tails"] = [], []
    for l in range(depth):
        if l < n_a:
            for grp in groups:
                grp["x"], tail = _mixer_a(grp["lay_a"], l, grp["x"], grp["mod"], grp["conv_state"],
                                          g_pre_mix, g_post_mix, w_a_in, w_a_dw, b_a_dw, ln_a_g, ln_a_b,
                                          w_a_out)
                grp["conv_tails"].append(tail)
        else:
            j = l - n_a
            q_p, q_s = (_q_proj(grp["lay_d"], l, j, grp["x"], grp["mod"], g_pre_mix, w_q,
                                dh ** -0.5 * LOG2E) for grp in groups)
            (_, _, kb_p, vb_p), (k_new, v_new) = groups[0]["kv"], groups[1]["kv"]
            outs = _attention(q_p, kb_p, vb_p, q_s, k_new, v_new, *past_kv, page_table, b_sb[j], n_heads)
            for grp, o in zip(groups, outs):
                grp["x"] = _o_proj(grp["lay_d"], l, j, o, grp["x"], grp["mod"], g_post_mix, w_o)
        for grp in groups:
            grp["x"], tail = _ffn(grp["lay_f"], l, grp["x"], grp["mod"], grp["ffn_state"], g_pre_ffn,
                                  g_post_ffn, w_ffn_up, w_ffn_dw, b_ffn_dw, w_ffn_down)
            grp["ffn_tails"].append(tail)
            if l == n_a - 1:
                grp["kv"] = _kv_proj(grp["lay_d"], grp["x"], g_kv, w_kv)
    results = []
    for grp in groups:
        x = grp["x"]
        shp = x.shape[:2] + (n_heads, dh)
        results.append((x, jnp.stack(grp["conv_tails"]), jnp.stack(grp["ffn_tails"]),
                        grp["kv"][0].reshape(shp), grp["kv"][1].reshape(shp)))
    (yp, cap, ffp, kp, vp), (ys, cas, ffs, ks, vs) = results
    return (yp, ys, cap, cas, ffp, ffs, kp, vp, ks, vs)
```

```python
import functools

import jax
import jax.numpy as jnp
from jax import lax
from jax.experimental import pallas as pl
from jax.experimental.pallas import tpu as pltpu

F32 = jnp.float32
BF16 = jnp.bfloat16
EPS = 1e-6
LANES = 128
SUBLANES = 8
MXU_WIDTH = 256
CONV_PAD = 32
VMEM_LIMIT = 56 * 1024 * 1024
GELU_C = 0.7978845608028654
LOG2E = 1.4426950408889634
SIGN_BIT = 0x80000000


def _cparams(*sem):
    return pltpu.CompilerParams(dimension_semantics=sem, vmem_limit_bytes=VMEM_LIMIT)


def _const_spec(shape, layer=None):
    nd = len(shape)
    if layer is None:
        return pl.BlockSpec(shape, lambda *_: (0,) * nd, pipeline_mode=pl.Buffered(1))
    return pl.BlockSpec((None,) + tuple(shape), lambda *_: (layer,) + (0,) * nd,
                        pipeline_mode=pl.Buffered(1))


def _pick_tile(n, target):
    best = None
    for t in range(LANES, min(n, target) + 1, LANES):
        if n % t == 0:
            best = t
    assert best is not None
    return best


def _rms(x, g):
    ms = jnp.mean(x * x, axis=-1, keepdims=True)
    return x * lax.rsqrt(ms + EPS) * g


def _dot(a, b):
    return jnp.dot(a, b, preferred_element_type=F32)


def _dot_nt(a, b):
    return lax.dot_general(a, b, (((1,), (1,)), ((), ())), preferred_element_type=F32)


def _softplus2(z):
    bits = lax.bitcast_convert_type(z, jnp.uint32) | jnp.uint32(SIGN_BIT)
    neg_abs = lax.bitcast_convert_type(bits, F32)
    return jnp.maximum(z, 0.0) + jnp.log2(1.0 + jnp.exp2(neg_abs))


def _sb_parts(z, tri, causal=None):
    s = _softplus2(z)
    lsig = z - s
    if causal is not None:
        s = jnp.where(causal, s, 0.0)
    c = _dot(s, tri)
    return lsig - c, c[:, 0:1] + s[:, 0:1]


def _idiv(x, n):
    assert n & (n - 1) == 0, "power-of-two divisor"
    return lax.shift_right_logical(x, n.bit_length() - 1)


def _imod(x, n):
    assert n & (n - 1) == 0, "power-of-two divisor"
    return x & (n - 1)


def _ada_body(cp_ref, cs_ref, w_ref, b_ref, op_ref, os_ref):
    w = w_ref[...].astype(BF16)
    b = b_ref[...]
    for c_ref, o_ref in ((cp_ref, op_ref), (cs_ref, os_ref)):
        c = c_ref[...]
        a = (c * jax.nn.sigmoid(c)).astype(BF16)
        o_ref[...] = _dot(a, w) + b


def _ada(c_prompt, c_sample, w_ada, b_ada):
    depth, d, n = w_ada.shape
    tn = _pick_tile(n, 1024)
    bp, bs = c_prompt.shape[0], c_sample.shape[0]
    return pl.pallas_call(
        _ada_body,
        grid=(depth, n // tn),
        in_specs=[
            pl.BlockSpec((bp, d), lambda l, j: (0, 0)),
            pl.BlockSpec((bs, d), lambda l, j: (0, 0)),
            pl.BlockSpec((None, d, tn), lambda l, j: (l, 0, j)),
            pl.BlockSpec((None, 1, tn), lambda l, j: (l, 0, j)),
        ],
        out_specs=[
            pl.BlockSpec((None, bp, tn), lambda l, j: (l, 0, j)),
            pl.BlockSpec((None, bs, tn), lambda l, j: (l, 0, j)),
        ],
        out_shape=[jax.ShapeDtypeStruct((depth, bp, n), F32),
                   jax.ShapeDtypeStruct((depth, bs, n), F32)],
        compiler_params=_cparams("arbitrary", "arbitrary"),
        name="ada",
    )(c_prompt, c_sample, w_ada, b_ada.reshape(depth, 1, n))


class _Layout:
    def __init__(self, x_shape, prompt, row_tile, seq_tile):
        ng, rt, _ = x_shape
        self.prompt = prompt
        if prompt:
            self.g, self.r = 1, min(row_tile, rt)
            assert rt % self.r == 0
            self.grid = (ng, rt // self.r)
        else:
            self.g, self.r = min(seq_tile, ng), rt
            assert ng % self.g == 0
            self.grid = (ng // self.g, 1)
        self.nt = self.grid[1]

    def act(self, d):
        return pl.BlockSpec((self.g, self.r, d), lambda i, t: (i, t, 0))

    def per_group(self, rows, d, layer=None):
        if layer is None:
            return pl.BlockSpec((self.g, rows, d), lambda i, t: (i, 0, 0))
        return pl.BlockSpec((None, self.g, rows, d), lambda i, t: (layer, i, 0, 0))


def _dwconv_rows(xs, w_ref, b_ref, out, r, taps, cs):
    off0 = CONV_PAD - (taps - 1)
    rc = min(r, 64)
    bias = b_ref[:, cs]
    for r0 in range(0, r, rc):
        acc = jnp.broadcast_to(bias, (rc, LANES))
        for res in range(SUBLANES):
            z = None
            for k in range(taps):
                if (k + off0) % SUBLANES != res:
                    continue
                base = r0 + k + off0 - res
                term = w_ref[k:k + 1, cs] * xs[0, base:base + rc + SUBLANES, cs]
                z = term if z is None else z + term
            if z is not None:
                acc = acc + z[res:res + rc]
        out[0, r0:r0 + rc, cs] = acc


def _dwconv_groups(xs, w_ref, b_ref, out, g, r, d, taps):
    off0 = CONV_PAD - (taps - 1)
    gc = min(g, 4)

    def strip(j, carry):
        cs = pl.ds(pl.multiple_of(j * LANES, LANES), LANES)
        bias = b_ref[:, cs]
        for g0 in range(0, g, gc):
            acc = jnp.broadcast_to(bias, (gc, r, LANES))
            for k in range(taps):
                acc = acc + w_ref[k:k + 1, cs] * xs[g0:g0 + gc, off0 + k:off0 + k + r, cs]
            out[g0:g0 + gc, :, cs] = acc
        return carry

    lax.fori_loop(0, d // LANES, strip, 0)


def _mixer_a_body(lay, d, taps, *refs):
    if lay.prompt:
        (x_ref, mod_ref, gpre, gpost, win, wdw, bdw, lng, lnb, wout,
         o_ref, tail_ref, xs, cv) = refs
        st_ref = None
    else:
        (x_ref, mod_ref, st_ref, gpre, gpost, win, wdw, bdw, lng, lnb, wout,
         o_ref, tail_ref, xs, cv) = refs
    g, r = lay.g, lay.r
    m = g * r
    t = pl.program_id(1)
    if lay.prompt:
        @pl.when(t == 0)
        def _():
            xs[:, 0:CONV_PAD, :] = jnp.zeros((g, CONV_PAD, d), F32)
            xs[:, r + CONV_PAD:, :] = jnp.zeros((g, SUBLANES, d), F32)
    else:
        xs[:, CONV_PAD - (taps - 1):CONV_PAD, :] = st_ref[...]

    x3 = x_ref[...]
    shift, scale, gate = mod_ref[:, 0:1, :], mod_ref[:, 1:2, :], mod_ref[:, 2:3, :]
    h3 = _rms(x3, gpre[...]) * (1.0 + scale) + shift
    hb = h3.reshape(m, d).astype(BF16)
    if lay.prompt:
        for j0 in range(0, d, MXU_WIDTH):
            a = _dot(hb, win[:, j0:j0 + MXU_WIDTH])
            gt = _dot(hb, win[:, d + j0:d + j0 + MXU_WIDTH])
            xs[:, CONV_PAD:CONV_PAD + r, j0:j0 + MXU_WIDTH] = (a * jax.nn.sigmoid(gt)).reshape(g, r, -1)
            for c0 in range(j0, j0 + MXU_WIDTH, LANES):
                _dwconv_rows(xs, wdw, bdw, cv, r, taps, slice(c0, c0 + LANES))
    else:
        ag = _dot(hb, win[...])
        u = ag[:, :d] * jax.nn.sigmoid(ag[:, d:])
        xs[:, CONV_PAD:CONV_PAD + r, :] = u.reshape(g, r, d)
        _dwconv_groups(xs, wdw, bdw, cv, g, r, d, taps)
    c3 = cv[...]
    mu = jnp.mean(c3, axis=-1, keepdims=True)
    dc = c3 - mu
    var = jnp.mean(dc * dc, axis=-1, keepdims=True)
    yn = dc * lax.rsqrt(var + EPS) * lng[...] + lnb[...]
    s = yn * jax.nn.sigmoid(yn)
    y = _dot(s.reshape(m, d).astype(BF16), wout[...])
    o_ref[...] = x3 + gate * _rms(y.reshape(g, r, d), gpost[...])

    lo = r + CONV_PAD - (taps - 1)
    if lay.prompt:
        @pl.when(t == lay.nt - 1)
        def _():
            tail_ref[...] = xs[:, lo:r + CONV_PAD, :]
        xs[:, 0:CONV_PAD, :] = xs[:, r:r + CONV_PAD, :]
    else:
        tail_ref[...] = xs[:, lo:r + CONV_PAD, :]


def _mixer_a(lay, l, x, mod, state, gpre, gpost, win, wdw, bdw, lng, lnb, wout):
    ng, rt, d = x.shape
    taps = wdw.shape[1]
    in_specs = [lay.act(d), lay.per_group(6, d, l)]
    args = [x, mod]
    if not lay.prompt:
        in_specs.append(lay.per_group(taps - 1, d, l))
        args.append(state)
    in_specs += [_const_spec((1, d), l), _const_spec((1, d), l), _const_spec((d, 2 * d), l),
                 _const_spec((taps, d), l), _const_spec((1, d), l), _const_spec((1, d), l),
                 _const_spec((1, d), l), _const_spec((d, d), l)]
    args += [gpre, gpost, win, wdw, bdw, lng, lnb, wout]
    return pl.pallas_call(
        functools.partial(_mixer_a_body, lay, d, taps),
        grid=lay.grid,
        in_specs=in_specs,
        out_specs=[lay.act(d), lay.per_group(taps - 1, d)],
        out_shape=[jax.ShapeDtypeStruct(x.shape, F32),
                   jax.ShapeDtypeStruct((ng, taps - 1, d), F32)],
        scratch_shapes=[pltpu.VMEM((lay.g, lay.r + CONV_PAD + (SUBLANES if lay.prompt else 0), d), F32),
                        pltpu.VMEM((lay.g, lay.r, d), F32)],
        compiler_params=_cparams("arbitrary", "arbitrary"),
        name="mixer_a_p" if lay.prompt else "mixer_a_s",
    )(*args)


def _ffn_body(lay, d, dff, chunks, *refs):
    if lay.prompt:
        (x_ref, mod_ref, gpre, gpost, wup, wdw, bdw, wdown,
         o_ref, tail_ref, gbuf, acc) = refs
        st_ref = None
    else:
        (x_ref, mod_ref, st_ref, gpre, gpost, wup, wdw, bdw, wdown,
         o_ref, tail_ref, gbuf, acc) = refs
    g, r = lay.g, lay.r
    m = g * r
    t = pl.program_id(1)
    if lay.prompt:
        @pl.when(t == 0)
        def _():
            gbuf[:, 0:SUBLANES, :] = jnp.zeros((g, SUBLANES, dff), F32)
    else:
        gbuf[:, SUBLANES - 2:SUBLANES, :] = st_ref[...]

    x3 = x_ref[...]
    shift, scale, gate = mod_ref[:, 3:4, :], mod_ref[:, 4:5, :], mod_ref[:, 5:6, :]
    hb = (_rms(x3, gpre[...]) * (1.0 + scale) + shift).reshape(m, d).astype(BF16)
    for c, (lo, fc) in enumerate(chunks):
        cs = slice(lo, lo + fc)
        gt = _dot(hb, wup[:, lo:lo + fc]).reshape(g, r, fc)
        val = _dot(hb, wup[:, dff + lo:dff + lo + fc]).reshape(g, r, fc)
        gbuf[:, SUBLANES:SUBLANES + r, cs] = gt
        g1 = gbuf[:, SUBLANES - 1:SUBLANES - 1 + r, cs]
        g2 = gbuf[:, SUBLANES - 2:SUBLANES - 2 + r, cs]
        y = wdw[0:1, cs] * g2 + wdw[1:2, cs] * g1 + wdw[2:3, cs] * gt + bdw[:, cs]
        act = 0.5 * y * (1.0 + jnp.tanh(GELU_C * (y + 0.044715 * (y * y * y)))) * val
        contrib = _dot(act.reshape(m, fc).astype(BF16), wdown[cs, :])
        if c == 0:
            acc[...] = contrib
        else:
            acc[...] += contrib
    o_ref[...] = x3 + gate * _rms(acc[...].reshape(g, r, d), gpost[...])

    if lay.prompt:
        @pl.when(t == lay.nt - 1)
        def _():
            tail_ref[...] = gbuf[:, r + SUBLANES - 2:r + SUBLANES, :]
        gbuf[:, 0:SUBLANES, :] = gbuf[:, r:r + SUBLANES, :]
    else:
        tail_ref[...] = gbuf[:, r + SUBLANES - 2:r + SUBLANES, :]


FFN_CHUNK = 768

def _ffn_chunks(dff):
    assert dff % 256 == 0
    return tuple((lo, min(FFN_CHUNK, dff - lo)) for lo in range(0, dff, FFN_CHUNK))


def _ffn(lay, l, x, mod, state, gpre, gpost, wup, wdw, bdw, wdown):
    ng, rt, d = x.shape
    dff = wdown.shape[1]
    chunks = _ffn_chunks(dff)
    in_specs = [lay.act(d), lay.per_group(6, d, l)]
    args = [x, mod]
    if not lay.prompt:
        in_specs.append(lay.per_group(2, dff, l))
        args.append(state)
    in_specs += [_const_spec((1, d), l), _const_spec((1, d), l), _const_spec((d, 2 * dff), l),
                 _const_spec((3, dff), l), _const_spec((1, dff), l), _const_spec((dff, d), l)]
    args += [gpre, gpost, wup, wdw, bdw, wdown]
    return pl.pallas_call(
        functools.partial(_ffn_body, lay, d, dff, chunks),
        grid=lay.grid,
        in_specs=in_specs,
        out_specs=[lay.act(d), lay.per_group(2, dff)],
        out_shape=[jax.ShapeDtypeStruct(x.shape, F32),
                   jax.ShapeDtypeStruct((ng, 2, dff), F32)],
        scratch_shapes=[pltpu.VMEM((lay.g, lay.r + SUBLANES, dff), F32),
                        pltpu.VMEM((lay.g * lay.r, d), F32)],
        compiler_params=_cparams("arbitrary", "arbitrary"),
        name="ffn_p" if lay.prompt else "ffn_s",
    )(*args)


def _kv_body(lay, d, hd, x_ref, g_ref, w_ref, k_ref, v_ref, *bf16_refs):
    m = lay.g * lay.r
    h = _rms(x_ref[...], g_ref[...]).reshape(m, d).astype(BF16)
    kv = _dot(h, w_ref[...])
    k = kv[:, :hd].reshape(lay.g, lay.r, hd)
    v = kv[:, hd:].reshape(lay.g, lay.r, hd)
    k_ref[...] = k
    v_ref[...] = v
    if bf16_refs:
        bf16_refs[0][...] = k.astype(BF16)
        bf16_refs[1][...] = v.astype(BF16)


def _kv_proj(lay, x, g_kv, w_kv):
    ng, rt, d = x.shape
    hd = w_kv.shape[1] // 2
    sd = jax.ShapeDtypeStruct
    out_shape = [sd((ng, rt, hd), F32), sd((ng, rt, hd), F32)]
    if lay.prompt:
        out_shape += [sd((ng, rt, hd), BF16), sd((ng, rt, hd), BF16)]
    return pl.pallas_call(
        functools.partial(_kv_body, lay, d, hd),
        grid=lay.grid,
        in_specs=[lay.act(d), _const_spec((1, d)), _const_spec((d, 2 * hd))],
        out_specs=[lay.act(hd)] * len(out_shape),
        out_shape=out_shape,
        compiler_params=_cparams("arbitrary", "arbitrary"),
        name="kv_p" if lay.prompt else "kv_s",
    )(x, g_kv, w_kv)


def _q_body(lay, d, hd, qscale, x_ref, mod_ref, gpre, w_ref, q_ref):
    m = lay.g * lay.r
    shift, scale = mod_ref[:, 0:1, :], mod_ref[:, 1:2, :]
    h = (_rms(x_ref[...], gpre[...]) * (1.0 + scale) + shift).reshape(m, d).astype(BF16)
    q = _dot(h, w_ref[...]) * qscale
    q_ref[...] = q.reshape(lay.g, lay.r, hd).astype(q_ref.dtype)


def _q_proj(lay, l, j, x, mod, gpre, w_q, qscale):
    ng, rt, d = x.shape
    hd = w_q.shape[2]
    return pl.pallas_call(
        functools.partial(_q_body, lay, d, hd, qscale),
        grid=lay.grid,
        in_specs=[lay.act(d), lay.per_group(6, d, l), _const_spec((1, d), l),
                  _const_spec((d, hd), j)],
        out_specs=lay.act(hd),
        out_shape=jax.ShapeDtypeStruct((ng, rt, hd), BF16 if lay.prompt else F32),
        compiler_params=_cparams("arbitrary", "arbitrary"),
        name="q_p" if lay.prompt else "q_s",
    )(x, mod, gpre, w_q)


def _o_body(lay, d, hd, o_ref, x_ref, mod_ref, gpost, w_ref, out_ref):
    m = lay.g * lay.r
    gate = mod_ref[:, 2:3, :]
    y = _dot(o_ref[...].reshape(m, hd).astype(BF16), w_ref[...])
    out_ref[...] = x_ref[...] + gate * _rms(y.reshape(lay.g, lay.r, d), gpost[...])


def _o_proj(lay, l, j, o, x, mod, gpost, w_o):
    ng, rt, d = x.shape
    hd = w_o.shape[1]
    return pl.pallas_call(
        functools.partial(_o_body, lay, d, hd),
        grid=lay.grid,
        in_specs=[lay.act(hd), lay.act(d), lay.per_group(6, d, l), _const_spec((1, d), l),
                  _const_spec((hd, d), j)],
        out_specs=lay.act(d),
        out_shape=jax.ShapeDtypeStruct(x.shape, F32),
        compiler_params=_cparams("arbitrary", "arbitrary"),
        name="o_p" if lay.prompt else "o_s",
    )(o, x, mod, gpost, w_o)


def _attn_p_body(tq, kb, dh, npair, bias_ref, q_ref, k_ref, v_ref, tri_ref, o_ref, acc):
    grp = pl.program_id(1)
    i = pl.program_id(2)
    nsub = tq // kb
    lane = lax.broadcasted_iota(jnp.int32, (tq, LANES), 1)
    tri = tri_ref[...]
    row = _imod(lax.broadcasted_iota(jnp.int32, (2 * tq, kb), 0), tq)
    col = lax.broadcasted_iota(jnp.int32, (2 * tq, kb), 1)
    qs, bias = [], []
    for p in range(npair):
        qp = q_ref[:, p * LANES:(p + 1) * LANES]
        zero = jnp.zeros_like(qp)
        qs.append(jnp.concatenate([jnp.where(lane < dh, qp, zero), jnp.where(lane >= dh, qp, zero)],
                                  axis=0))
        h0 = (grp * npair + p) * 2
        bias.append((bias_ref[h0] * LOG2E, bias_ref[h0 + 1] * LOG2E))

    def block(p, j, carry, sub):
        ks = pl.ds(pl.multiple_of(j * kb, kb), kb)
        ls = slice(p * LANES, (p + 1) * LANES)
        zz = _dot_nt(qs[p], k_ref[ks, ls])
        z = jnp.concatenate([zz[:tq] + bias[p][0], zz[tq:] + bias[p][1]], axis=0)
        causal = None if sub is None else col + sub * kb < row
        e, tot = _sb_parts(z, tri, causal)
        att = jnp.exp2(e - carry)
        if sub is not None:
            att = jnp.where(causal, att, 0.0)
        acc[p] += _dot(att, v_ref[ks, ls])
        return carry + tot

    def first_block(p):
        r0 = tq - kb
        ks = pl.ds(pl.multiple_of((i * nsub + nsub - 1) * kb, kb), kb)
        ls = slice(p * LANES, (p + 1) * LANES)
        q2 = jnp.concatenate([qs[p][r0:tq], qs[p][tq + r0:2 * tq]], axis=0)
        zz = _dot_nt(q2, k_ref[ks, ls])
        z = jnp.concatenate([zz[:kb] + bias[p][0], zz[kb:] + bias[p][1]], axis=0)
        causal = (lax.broadcasted_iota(jnp.int32, (2 * kb, kb), 1)
                  < _imod(lax.broadcasted_iota(jnp.int32, (2 * kb, kb), 0), kb))
        e, tot = _sb_parts(z, tri, causal)
        att = jnp.where(causal, jnp.exp2(e), 0.0)
        pv = _dot(att, v_ref[ks, ls])
        if r0 == 0:
            acc[p] = pv
            return tot
        zero = jnp.zeros((r0, LANES), F32)
        acc[p] = jnp.concatenate([zero, pv[:kb], zero, pv[kb:]], axis=0)
        t0 = jnp.broadcast_to(tot, (2 * kb, LANES))
        return jnp.concatenate([zero, t0[:kb], zero, t0[kb:]], axis=0)[:, 0:1]

    carries = [first_block(p) for p in range(npair)]
    for sub in range(nsub - 2, -1, -1):
        for p in range(npair):
            carries[p] = block(p, i * nsub + sub, carries[p], sub)

    def step(s, carry):
        carry = list(carry)
        for u in range(nsub):
            for p in range(npair):
                carry[p] = block(p, (i - s) * nsub - 1 - u, carry[p], None)
        return tuple(carry)

    lax.fori_loop(0, i, step, tuple(carries))
    for p in range(npair):
        o_ref[:, p * LANES:(p + 1) * LANES] = jnp.where(
            lane < dh, acc[p, 0:tq, :], acc[p, tq:2 * tq, :]).astype(o_ref.dtype)


def _attn_s_one(ppb, page, r, n_heads, dh, bias_ref, q, kn_rows, vn_rows, tri, k_refs, v_refs):
    n_pages = len(k_refs)
    kb = ppb * page
    hd = n_heads * dh
    hq = n_heads * r
    row_h = _idiv(lax.broadcasted_iota(jnp.int32, (hq, hd), 0), r)
    lane_h = _idiv(lax.broadcasted_iota(jnp.int32, (hq, hd), 1), dh)
    own = row_h == lane_h
    qrep = jnp.broadcast_to(q[None], (n_heads, r, hd)).reshape(hq, hd)
    qbd = jnp.where(own, qrep, 0.0).astype(BF16)
    rh = _idiv(lax.broadcasted_iota(jnp.int32, (hq, kb), 0), r)
    bias = jnp.zeros((hq, kb), F32)
    for h in range(n_heads):
        bias = jnp.where(rh == h, bias_ref[h] * LOG2E, bias)

    qi = _imod(lax.broadcasted_iota(jnp.int32, (hq, LANES), 0), r)
    kj = lax.broadcasted_iota(jnp.int32, (hq, LANES), 1)
    causal = kj < qi
    pad = jnp.zeros((LANES - r, hd), F32)
    kn = jnp.concatenate([kn_rows, pad], axis=0).astype(BF16)
    vn = jnp.concatenate([vn_rows, pad], axis=0).astype(BF16)
    z = _dot_nt(qbd, kn) + bias[:, :LANES]
    e, carry = _sb_parts(z, tri[:LANES, :LANES], causal)
    att_new = jnp.where(causal, jnp.exp2(e), 0.0)
    nb = n_pages // ppb

    def block_t(refs, j):
        pages = [refs[j * ppb + u][...] for u in range(ppb)]
        return jnp.concatenate(pages, axis=1).astype(BF16)

    parts = [_sb_parts(_dot(qbd, block_t(k_refs, j)) + bias, tri) for j in range(nb)]
    acc = _dot(att_new, vn)
    for j in range(nb - 1, -1, -1):
        e, tot = parts[j]
        acc = acc + _dot_nt(jnp.exp2(e - carry), block_t(v_refs, j))
        carry = carry + tot
    o = jnp.where(own, acc, 0.0).reshape(n_heads, r, hd)
    return jnp.sum(o, axis=0)


def _attn_body(cfg, pt_ref, bias_ref, q_ref, k_ref, v_ref, tri_ref, qs_ref, kn_ref, vn_ref, *refs):
    del pt_ref
    tq, kb, dh, npair, n_heads, r, spp, n_pages, ppb, page = cfg
    npg = spp * n_pages
    k_refs, v_refs = refs[:npg], refs[npg:2 * npg]
    o_ref, os_ref, acc = refs[2 * npg:]
    _attn_p_body(tq, kb, dh, npair, bias_ref, q_ref, k_ref, v_ref, tri_ref, o_ref, acc)
    tri = tri_ref[...]
    for u in range(spp):
        pages = slice(u * n_pages, (u + 1) * n_pages)
        os_ref[u] = _attn_s_one(ppb, page, r, n_heads, dh, bias_ref, qs_ref[u], kn_ref[u], vn_ref[u],
                                tri, k_refs[pages], v_refs[pages])


def _attention(q_p, kb_p, vb_p, q_s, k_new, v_new, cache_kt, cache_vt, page_table, bias, n_heads):
    b, t, hd = q_p.shape
    ns, r, _ = q_s.shape
    dh = hd // n_heads
    assert 2 * dh == LANES and n_heads % 2 == 0
    npair = 2 if n_heads % 4 == 0 else 1
    kb = min(256, t)
    tq = min(2 * kb, t)
    assert t % tq == 0 and tq % kb == 0
    width = npair * LANES
    n_grp, nq = n_heads // (2 * npair), t // tq
    steps = b * n_grp * nq
    page = cache_kt.shape[2]
    n_pages = page_table.shape[1]
    ppb = max(1, kb // page)
    assert ppb * page == kb and n_pages % ppb == 0 and page % LANES == 0 and r <= LANES
    assert ns % steps == 0, "sample sequences are spread evenly over the prompt grid steps"
    spp = ns // steps
    tri = (jnp.arange(kb)[:, None] > jnp.arange(kb)[None, :]).astype(BF16)

    def step(bi, g, i):
        return (bi * n_grp + g) * nq + i

    seq = pl.BlockSpec((spp, r, hd), lambda bi, g, i, pt: (step(bi, g, i), 0, 0))

    def page_spec(u, p):
        return pl.BlockSpec((None, hd, page),
                            lambda bi, g, i, pt: (pt[(step(bi, g, i) * spp + u) * n_pages + p], 0, 0))

    pages = [page_spec(u, p) for u in range(spp) for p in range(n_pages)]
    tile = pl.BlockSpec((None, tq, width), lambda bi, g, i, pt: (bi, i, g))
    whole = pl.BlockSpec((None, t, width), lambda bi, g, i, pt: (bi, 0, g))
    grid_spec = pltpu.PrefetchScalarGridSpec(
        num_scalar_prefetch=1,
        grid=(b, n_grp, nq),
        in_specs=[pl.BlockSpec(memory_space=pltpu.SMEM), tile, whole, whole,
                  pl.BlockSpec((kb, kb), lambda bi, g, i, pt: (0, 0)), seq, seq, seq] + pages * 2,
        out_specs=[tile, seq],
        scratch_shapes=[pltpu.VMEM((npair, 2 * tq, LANES), F32)],
    )
    cfg = (tq, kb, dh, npair, n_heads, r, spp, n_pages, ppb, page)
    return pl.pallas_call(
        functools.partial(_attn_body, cfg),
        grid_spec=grid_spec,
        out_shape=[jax.ShapeDtypeStruct((b, t, hd), BF16), jax.ShapeDtypeStruct((ns, r, hd), F32)],
        compiler_params=_cparams("arbitrary", "arbitrary", "arbitrary"),
        name="attn",
    )(page_table.reshape(-1), bias, q_p, kb_p, vb_p, tri, q_s, k_new, v_new,
      *([cache_kt] * (spp * n_pages)), *([cache_vt] * (spp * n_pages)))


def kernel(x_prompt, x_sample, c_prompt, c_sample, state_conv_a, state_ffn_conv, cache_k, cache_v, page_table, w_ada, b_ada, g_pre_mix, g_post_mix, g_pre_ffn, g_post_ffn, w_a_in, w_a_dw, b_a_dw, ln_a_g, ln_a_b, w_a_out, g_kv, w_kv, w_q, w_o, b_sb, w_ffn_up, w_ffn_dw, b_ffn_dw, w_ffn_down):
    depth, d = g_pre_mix.shape
    n_a = w_a_in.shape[0]
    n_heads, dh = cache_k.shape[2], cache_k.shape[3]
    hd = n_heads * dh
    dff = w_ffn_down.shape[1]

    def vec(a):
        return a.reshape(a.shape[0], 1, a.shape[1])

    g_pre_mix, g_post_mix, g_pre_ffn, g_post_ffn = map(vec, (g_pre_mix, g_post_mix, g_pre_ffn, g_post_ffn))
    b_a_dw, ln_a_g, ln_a_b, b_ffn_dw = map(vec, (b_a_dw, ln_a_g, ln_a_b, b_ffn_dw))
    w_a_in, w_a_out, w_kv, w_q, w_o, w_ffn_up, w_ffn_down = (
        w.astype(BF16) for w in (w_a_in, w_a_out, w_kv, w_q, w_o, w_ffn_up, w_ffn_down))
    g_kv = g_kv.reshape(1, d)

    mod_p, mod_s = _ada(c_prompt, c_sample, w_ada, b_ada)
    n_phys, page = cache_k.shape[:2]
    past_kv = tuple(c.transpose(0, 2, 3, 1).reshape(n_phys, hd, page) for c in (cache_k, cache_v))
    groups = [
        dict(x=x_prompt, mod=mod_p.reshape(depth, -1, 6, d), conv_state=None, ffn_state=None,
             lay_a=_Layout(x_prompt.shape, True, 512, 0), lay_f=_Layout(x_prompt.shape, True, 512, 0),
             lay_d=_Layout(x_prompt.shape, True, 512, 0)),
        dict(x=x_sample, mod=mod_s.reshape(depth, -1, 6, d), conv_state=state_conv_a,
             ffn_state=state_ffn_conv,
             lay_a=_Layout(x_sample.shape, False, 0, 32), lay_f=_Layout(x_sample.shape, False, 0, 32),
             lay_d=_Layout(x_sample.shape, False, 0, 64)),
    ]
    for grp in groups:
        grp["conv_tails"], grp["ffn_tails"] = [], []
    for l in range(depth):
        if l < n_a:
            for grp in groups:
                grp["x"], tail = _mixer_a(grp["lay_a"], l, grp["x"], grp["mod"], grp["conv_state"],
                                          g_pre_mix, g_post_mix, w_a_in, w_a_dw, b_a_dw, ln_a_g, ln_a_b,
                                          w_a_out)
                grp["conv_tails"].append(tail)
        else:
            j = l - n_a
            q_p, q_s = (_q_proj(grp["lay_d"], l, j, grp["x"], grp["mod"], g_pre_mix, w_q,
                                dh ** -0.5 * LOG2E) for grp in groups)
            (_, _, kb_p, vb_p), (k_new, v_new) = groups[0]["kv"], groups[1]["kv"]
            outs = _attention(q_p, kb_p, vb_p, q_s, k_new, v_new, *past_kv, page_table, b_sb[j], n_heads)
            for grp, o in zip(groups, outs):
                grp["x"] = _o_proj(grp["lay_d"], l, j, o, grp["x"], grp["mod"], g_post_mix, w_o)
        for grp in groups:
            grp["x"], tail = _ffn(grp["lay_f"], l, grp["x"], grp["mod"], grp["ffn_state"], g_pre_ffn,
                                  g_post_ffn, w_ffn_up, w_ffn_dw, b_ffn_dw, w_ffn_down)
            grp["ffn_tails"].append(tail)
            if l == n_a - 1:
                grp["kv"] = _kv_proj(grp["lay_d"], grp["x"], g_kv, w_kv)
    results = []
    for grp in groups:
        x = grp["x"]
        shp = x.shape[:2] + (n_heads, dh)
        results.append((x, jnp.stack(grp["conv_tails"]), jnp.stack(grp["ffn_tails"]),
                        grp["kv"][0].reshape(shp), grp["kv"][1].reshape(shp)))
    (yp, cap, ffp, kp, vp), (ys, cas, ffs, ks, vs) = results
    return (yp, ys, cap, cas, ffp, ffs, kp, vp, ks, vs)
```

```python
import functools

import jax
import jax.numpy as jnp
from jax import lax
from jax.experimental import pallas as pl
from jax.experimental.pallas import tpu as pltpu

F32 = jnp.float32
BF16 = jnp.bfloat16
EPS = 1e-6
LANES = 128
SUBLANES = 8
MXU_WIDTH = 256
CONV_PAD = 32
VMEM_LIMIT = 56 * 1024 * 1024
GELU_C = 0.7978845608028654
LOG2E = 1.4426950408889634
SIGN_BIT = 0x80000000


def _cparams(*sem):
    return pltpu.CompilerParams(dimension_semantics=sem, vmem_limit_bytes=VMEM_LIMIT)


def _const_spec(shape, layer=None):
    nd = len(shape)
    if layer is None:
        return pl.BlockSpec(shape, lambda *_: (0,) * nd, pipeline_mode=pl.Buffered(1))
    return pl.BlockSpec((None,) + tuple(shape), lambda *_: (layer,) + (0,) * nd,
                        pipeline_mode=pl.Buffered(1))


def _pick_tile(n, target):
    best = None
    for t in range(LANES, min(n, target) + 1, LANES):
        if n % t == 0:
            best = t
    assert best is not None
    return best


def _rms(x, g):
    ms = jnp.mean(x * x, axis=-1, keepdims=True)
    return x * lax.rsqrt(ms + EPS) * g


def _dot(a, b):
    return jnp.dot(a, b, preferred_element_type=F32)


def _dot_nt(a, b):
    return lax.dot_general(a, b, (((1,), (1,)), ((), ())), preferred_element_type=F32)


def _softplus2(z):
    bits = lax.bitcast_convert_type(z, jnp.uint32) | jnp.uint32(SIGN_BIT)
    neg_abs = lax.bitcast_convert_type(bits, F32)
    return jnp.maximum(z, 0.0) + jnp.log2(1.0 + jnp.exp2(neg_abs))


def _sb_parts(z, tri, causal=None):
    s = _softplus2(z)
    lsig = z - s
    if causal is not None:
        s = jnp.where(causal, s, 0.0)
    c = _dot(s, tri)
    return lsig - c, c[:, 0:1] + s[:, 0:1]


def _idiv(x, n):
    assert n & (n - 1) == 0, "power-of-two divisor"
    return lax.shift_right_logical(x, n.bit_length() - 1)


def _imod(x, n):
    assert n & (n - 1) == 0, "power-of-two divisor"
    return x & (n - 1)


def _ada_body(cp_ref, cs_ref, w_ref, b_ref, op_ref, os_ref):
    w = w_ref[...].astype(BF16)
    b = b_ref[...]
    for c_ref, o_ref in ((cp_ref, op_ref), (cs_ref, os_ref)):
        c = c_ref[...]
        a = (c * jax.nn.sigmoid(c)).astype(BF16)
        o_ref[...] = _dot(a, w) + b


def _ada(c_prompt, c_sample, w_ada, b_ada):
    depth, d, n = w_ada.shape
    tn = _pick_tile(n, 1024)
    bp, bs = c_prompt.shape[0], c_sample.shape[0]
    return pl.pallas_call(
        _ada_body,
        grid=(depth, n // tn),
        in_specs=[
            pl.BlockSpec((bp, d), lambda l, j: (0, 0)),
            pl.BlockSpec((bs, d), lambda l, j: (0, 0)),
            pl.BlockSpec((None, d, tn), lambda l, j: (l, 0, j)),
            pl.BlockSpec((None, 1, tn), lambda l, j: (l, 0, j)),
        ],
        out_specs=[
            pl.BlockSpec((None, bp, tn), lambda l, j: (l, 0, j)),
            pl.BlockSpec((None, bs, tn), lambda l, j: (l, 0, j)),
        ],
        out_shape=[jax.ShapeDtypeStruct((depth, bp, n), F32),
                   jax.ShapeDtypeStruct((depth, bs, n), F32)],
        compiler_params=_cparams("arbitrary", "arbitrary"),
        name="ada",
    )(c_prompt, c_sample, w_ada, b_ada.reshape(depth, 1, n))


class _Layout:
    def __init__(self, x_shape, prompt, row_tile, seq_tile):
        ng, rt, _ = x_shape
        self.prompt = prompt
        if prompt:
            self.g, self.r = 1, min(row_tile, rt)
            assert rt % self.r == 0
            self.grid = (ng, rt // self.r)
        else:
            self.g, self.r = min(seq_tile, ng), rt
            assert ng % self.g == 0
            self.grid = (ng // self.g, 1)
        self.nt = self.grid[1]

    def act(self, d):
        return pl.BlockSpec((self.g, self.r, d), lambda i, t: (i, t, 0))

    def per_group(self, rows, d, layer=None):
        if layer is None:
            return pl.BlockSpec((self.g, rows, d), lambda i, t: (i, 0, 0))
        return pl.BlockSpec((None, self.g, rows, d), lambda i, t: (layer, i, 0, 0))


def _dwconv_rows(xs, w_ref, b_ref, out, r, taps, cs):
    off0 = CONV_PAD - (taps - 1)
    rc = min(r, 64)
    bias = b_ref[:, cs]
    for r0 in range(0, r, rc):
        acc = jnp.broadcast_to(bias, (rc, LANES))
        for res in range(SUBLANES):
            z = None
            for k in range(taps):
                if (k + off0) % SUBLANES != res:
                    continue
                base = r0 + k + off0 - res
                term = w_ref[k:k + 1, cs] * xs[0, base:base + rc + SUBLANES, cs]
                z = term if z is None else z + term
            if z is not None:
                acc = acc + z[res:res + rc]
        out[0, r0:r0 + rc, cs] = acc


def _dwconv_groups(xs, w_ref, b_ref, out, g, r, d, taps):
    off0 = CONV_PAD - (taps - 1)
    gc = min(g, 4)

    def strip(j, carry):
        cs = pl.ds(pl.multiple_of(j * LANES, LANES), LANES)
        bias = b_ref[:, cs]
        for g0 in range(0, g, gc):
            acc = jnp.broadcast_to(bias, (gc, r, LANES))
            for k in range(taps):
                acc = acc + w_ref[k:k + 1, cs] * xs[g0:g0 + gc, off0 + k:off0 + k + r, cs]
            out[g0:g0 + gc, :, cs] = acc
        return carry

    lax.fori_loop(0, d // LANES, strip, 0)


def _mixer_a_body(lay, d, taps, *refs):
    if lay.prompt:
        (x_ref, mod_ref, gpre, gpost, win, wdw, bdw, lng, lnb, wout,
         o_ref, tail_ref, xs, cv) = refs
        st_ref = None
    else:
        (x_ref, mod_ref, st_ref, gpre, gpost, win, wdw, bdw, lng, lnb, wout,
         o_ref, tail_ref, xs, cv) = refs
    g, r = lay.g, lay.r
    m = g * r
    t = pl.program_id(1)
    if lay.prompt:
        @pl.when(t == 0)
        def _():
            xs[:, 0:CONV_PAD, :] = jnp.zeros((g, CONV_PAD, d), F32)
            xs[:, r + CONV_PAD:, :] = jnp.zeros((g, SUBLANES, d), F32)
    else:
        xs[:, CONV_PAD - (taps - 1):CONV_PAD, :] = st_ref[...]

    x3 = x_ref[...]
    shift, scale, gate = mod_ref[:, 0:1, :], mod_ref[:, 1:2, :], mod_ref[:, 2:3, :]
    h3 = _rms(x3, gpre[...]) * (1.0 + scale) + shift
    hb = h3.reshape(m, d).astype(BF16)
    if lay.prompt:
        for j0 in range(0, d, MXU_WIDTH):
            a = _dot(hb, win[:, j0:j0 + MXU_WIDTH])
            gt = _dot(hb, win[:, d + j0:d + j0 + MXU_WIDTH])
            xs[:, CONV_PAD:CONV_PAD + r, j0:j0 + MXU_WIDTH] = (a * jax.nn.sigmoid(gt)).reshape(g, r, -1)
            for c0 in range(j0, j0 + MXU_WIDTH, LANES):
                _dwconv_rows(xs, wdw, bdw, cv, r, taps, slice(c0, c0 + LANES))
    else:
        ag = _dot(hb, win[...])
        u = ag[:, :d] * jax.nn.sigmoid(ag[:, d:])
        xs[:, CONV_PAD:CONV_PAD + r, :] = u.reshape(g, r, d)
        _dwconv_groups(xs, wdw, bdw, cv, g, r, d, taps)
    c3 = cv[...]
    mu = jnp.mean(c3, axis=-1, keepdims=True)
    dc = c3 - mu
    var = jnp.mean(dc * dc, axis=-1, keepdims=True)
    yn = dc * lax.rsqrt(var + EPS) * lng[...] + lnb[...]
    s = yn * jax.nn.sigmoid(yn)
    y = _dot(s.reshape(m, d).astype(BF16), wout[...])
    o_ref[...] = x3 + gate * _rms(y.reshape(g, r, d), gpost[...])

    lo = r + CONV_PAD - (taps - 1)
    if lay.prompt:
        @pl.when(t == lay.nt - 1)
        def _():
            tail_ref[...] = xs[:, lo:r + CONV_PAD, :]
        xs[:, 0:CONV_PAD, :] = xs[:, r:r + CONV_PAD, :]
    else:
        tail_ref[...] = xs[:, lo:r + CONV_PAD, :]


def _mixer_a(lay, l, x, mod, state, gpre, gpost, win, wdw, bdw, lng, lnb, wout):
    ng, rt, d = x.shape
    taps = wdw.shape[1]
    in_specs = [lay.act(d), lay.per_group(6, d, l)]
    args = [x, mod]
    if not lay.prompt:
        in_specs.append(lay.per_group(taps - 1, d, l))
        args.append(state)
    in_specs += [_const_spec((1, d), l), _const_spec((1, d), l), _const_spec((d, 2 * d), l),
                 _const_spec((taps, d), l), _const_spec((1, d), l), _const_spec((1, d), l),
                 _const_spec((1, d), l), _const_spec((d, d), l)]
    args += [gpre, gpost, win, wdw, bdw, lng, lnb, wout]
    return pl.pallas_call(
        functools.partial(_mixer_a_body, lay, d, taps),
        grid=lay.grid,
        in_specs=in_specs,
        out_specs=[lay.act(d), lay.per_group(taps - 1, d)],
        out_shape=[jax.ShapeDtypeStruct(x.shape, F32),
                   jax.ShapeDtypeStruct((ng, taps - 1, d), F32)],
        scratch_shapes=[pltpu.VMEM((lay.g, lay.r + CONV_PAD + (SUBLANES if lay.prompt else 0), d), F32),
                        pltpu.VMEM((lay.g, lay.r, d), F32)],
        compiler_params=_cparams("arbitrary", "arbitrary"),
        name="mixer_a_p" if lay.prompt else "mixer_a_s",
    )(*args)


def _ffn_body(lay, d, dff, chunks, *refs):
    if lay.prompt:
        (x_ref, mod_ref, gpre, gpost, wup, wdw, bdw, wdown,
         o_ref, tail_ref, gbuf, acc) = refs
        st_ref = None
    else:
        (x_ref, mod_ref, st_ref, gpre, gpost, wup, wdw, bdw, wdown,
         o_ref, tail_ref, gbuf, acc) = refs
    g, r = lay.g, lay.r
    m = g * r
    t = pl.program_id(1)
    if lay.prompt:
        @pl.when(t == 0)
        def _():
            gbuf[:, 0:SUBLANES, :] = jnp.zeros((g, SUBLANES, dff), F32)
    else:
        gbuf[:, SUBLANES - 2:SUBLANES, :] = st_ref[...]

    x3 = x_ref[...]
    shift, scale, gate = mod_ref[:, 3:4, :], mod_ref[:, 4:5, :], mod_ref[:, 5:6, :]
    hb = (_rms(x3, gpre[...]) * (1.0 + scale) + shift).reshape(m, d).astype(BF16)
    for c, (lo, fc) in enumerate(chunks):
        cs = slice(lo, lo + fc)
        gt = _dot(hb, wup[:, lo:lo + fc]).reshape(g, r, fc)
        val = _dot(hb, wup[:, dff + lo:dff + lo + fc]).reshape(g, r, fc)
        gbuf[:, SUBLANES:SUBLANES + r, cs] = gt
        g1 = gbuf[:, SUBLANES - 1:SUBLANES - 1 + r, cs]
        g2 = gbuf[:, SUBLANES - 2:SUBLANES - 2 + r, cs]
        y = wdw[0:1, cs] * g2 + wdw[1:2, cs] * g1 + wdw[2:3, cs] * gt + bdw[:, cs]
        act = 0.5 * y * (1.0 + jnp.tanh(GELU_C * (y + 0.044715 * (y * y * y)))) * val
        contrib = _dot(act.reshape(m, fc).astype(BF16), wdown[cs, :])
        if c == 0:
            acc[...] = contrib
        else:
            acc[...] += contrib
    o_ref[...] = x3 + gate * _rms(acc[...].reshape(g, r, d), gpost[...])

    if lay.prompt:
        @pl.when(t == lay.nt - 1)
        def _():
            tail_ref[...] = gbuf[:, r + SUBLANES - 2:r + SUBLANES, :]
        gbuf[:, 0:SUBLANES, :] = gbuf[:, r:r + SUBLANES, :]
    else:
        tail_ref[...] = gbuf[:, r + SUBLANES - 2:r + SUBLANES, :]


FFN_CHUNK = 768

def _ffn_chunks(dff):
    assert dff % 256 == 0
    return tuple((lo, min(FFN_CHUNK, dff - lo)) for lo in range(0, dff, FFN_CHUNK))


def _ffn(lay, l, x, mod, state, gpre, gpost, wup, wdw, bdw, wdown):
    ng, rt, d = x.shape
    dff = wdown.shape[1]
    chunks = _ffn_chunks(dff)
    in_specs = [lay.act(d), lay.per_group(6, d, l)]
    args = [x, mod]
    if not lay.prompt:
        in_specs.append(lay.per_group(2, dff, l))
        args.append(state)
    in_specs += [_const_spec((1, d), l), _const_spec((1, d), l), _const_spec((d, 2 * dff), l),
                 _const_spec((3, dff), l), _const_spec((1, dff), l), _const_spec((dff, d), l)]
    args += [gpre, gpost, wup, wdw, bdw, wdown]
    return pl.pallas_call(
        functools.partial(_ffn_body, lay, d, dff, chunks),
        grid=lay.grid,
        in_specs=in_specs,
        out_specs=[lay.act(d), lay.per_group(2, dff)],
        out_shape=[jax.ShapeDtypeStruct(x.shape, F32),
                   jax.ShapeDtypeStruct((ng, 2, dff), F32)],
        scratch_shapes=[pltpu.VMEM((lay.g, lay.r + SUBLANES, dff), F32),
                        pltpu.VMEM((lay.g * lay.r, d), F32)],
        compiler_params=_cparams("arbitrary", "arbitrary"),
        name="ffn_p" if lay.prompt else "ffn_s",
    )(*args)


def _kv_body(lay, d, hd, x_ref, g_ref, w_ref, k_ref, v_ref, *bf16_refs):
    m = lay.g * lay.r
    h = _rms(x_ref[...], g_ref[...]).reshape(m, d).astype(BF16)
    kv = _dot(h, w_ref[...])
    k = kv[:, :hd].reshape(lay.g, lay.r, hd)
    v = kv[:, hd:].reshape(lay.g, lay.r, hd)
    k_ref[...] = k
    v_ref[...] = v
    if bf16_refs:
        bf16_refs[0][...] = k.astype(BF16)
        bf16_refs[1][...] = v.astype(BF16)


def _kv_proj(lay, x, g_kv, w_kv):
    ng, rt, d = x.shape
    hd = w_kv.shape[1] // 2
    sd = jax.ShapeDtypeStruct
    out_shape = [sd((ng, rt, hd), F32), sd((ng, rt, hd), F32)]
    if lay.prompt:
        out_shape += [sd((ng, rt, hd), BF16), sd((ng, rt, hd), BF16)]
    return pl.pallas_call(
        functools.partial(_kv_body, lay, d, hd),
        grid=lay.grid,
        in_specs=[lay.act(d), _const_spec((1, d)), _const_spec((d, 2 * hd))],
        out_specs=[lay.act(hd)] * len(out_shape),
        out_shape=out_shape,
        compiler_params=_cparams("arbitrary", "arbitrary"),
        name="kv_p" if lay.prompt else "kv_s",
    )(x, g_kv, w_kv)


def _q_body(lay, d, hd, qscale, x_ref, mod_ref, gpre, w_ref, q_ref):
    m = lay.g * lay.r
    shift, scale = mod_ref[:, 0:1, :], mod_ref[:, 1:2, :]
    h = (_rms(x_ref[...], gpre[...]) * (1.0 + scale) + shift).reshape(m, d).astype(BF16)
    q = _dot(h, w_ref[...]) * qscale
    q_ref[...] = q.reshape(lay.g, lay.r, hd).astype(q_ref.dtype)


def _q_proj(lay, l, j, x, mod, gpre, w_q, qscale):
    ng, rt, d = x.shape
    hd = w_q.shape[2]
    return pl.pallas_call(
        functools.partial(_q_body, lay, d, hd, qscale),
        grid=lay.grid,
        in_specs=[lay.act(d), lay.per_group(6, d, l), _const_spec((1, d), l),
                  _const_spec((d, hd), j)],
        out_specs=lay.act(hd),
        out_shape=jax.ShapeDtypeStruct((ng, rt, hd), BF16 if lay.prompt else F32),
        compiler_params=_cparams("arbitrary", "arbitrary"),
        name="q_p" if lay.prompt else "q_s",
    )(x, mod, gpre, w_q)


def _o_body(lay, d, hd, o_ref, x_ref, mod_ref, gpost, w_ref, out_ref):
    m = lay.g * lay.r
    gate = mod_ref[:, 2:3, :]
    y = _dot(o_ref[...].reshape(m, hd).astype(BF16), w_ref[...])
    out_ref[...] = x_ref[...] + gate * _rms(y.reshape(lay.g, lay.r, d), gpost[...])


def _o_proj(lay, l, j, o, x, mod, gpost, w_o):
    ng, rt, d = x.shape
    hd = w_o.shape[1]
    return pl.pallas_call(
        functools.partial(_o_body, lay, d, hd),
        grid=lay.grid,
        in_specs=[lay.act(hd), lay.act(d), lay.per_group(6, d, l), _const_spec((1, d), l),
                  _const_spec((hd, d), j)],
        out_specs=lay.act(d),
        out_shape=jax.ShapeDtypeStruct(x.shape, F32),
        compiler_params=_cparams("arbitrary", "arbitrary"),
        name="o_p" if lay.prompt else "o_s",
    )(o, x, mod, gpost, w_o)


def _attn_p_body(tq, kb, dh, npair, bias_ref, q_ref, k_ref, v_ref, tri_ref, o_ref, acc):
    grp = pl.program_id(1)
    i = pl.program_id(2)
    nsub = tq // kb
    lane = lax.broadcasted_iota(jnp.int32, (tq, LANES), 1)
    tri = tri_ref[...]
    row = _imod(lax.broadcasted_iota(jnp.int32, (2 * tq, kb), 0), tq)
    col = lax.broadcasted_iota(jnp.int32, (2 * tq, kb), 1)
    qs, bias = [], []
    for p in range(npair):
        qp = q_ref[:, p * LANES:(p + 1) * LANES]
        zero = jnp.zeros_like(qp)
        qs.append(jnp.concatenate([jnp.where(lane < dh, qp, zero), jnp.where(lane >= dh, qp, zero)],
                                  axis=0))
        h0 = (grp * npair + p) * 2
        bias.append((bias_ref[h0] * LOG2E, bias_ref[h0 + 1] * LOG2E))

    def block(p, j, carry, sub):
        ks = pl.ds(pl.multiple_of(j * kb, kb), kb)
        ls = slice(p * LANES, (p + 1) * LANES)
        zz = _dot_nt(qs[p], k_ref[ks, ls])
        z = jnp.concatenate([zz[:tq] + bias[p][0], zz[tq:] + bias[p][1]], axis=0)
        causal = None if sub is None else col + sub * kb < row
        e, tot = _sb_parts(z, tri, causal)
        att = jnp.exp2(e - carry)
        if sub is not None:
            att = jnp.where(causal, att, 0.0)
        acc[p] += _dot(att, v_ref[ks, ls])
        return carry + tot

    def first_block(p):
        r0 = tq - kb
        ks = pl.ds(pl.multiple_of((i * nsub + nsub - 1) * kb, kb), kb)
        ls = slice(p * LANES, (p + 1) * LANES)
        q2 = jnp.concatenate([qs[p][r0:tq], qs[p][tq + r0:2 * tq]], axis=0)
        zz = _dot_nt(q2, k_ref[ks, ls])
        z = jnp.concatenate([zz[:kb] + bias[p][0], zz[kb:] + bias[p][1]], axis=0)
        causal = (lax.broadcasted_iota(jnp.int32, (2 * kb, kb), 1)
                  < _imod(lax.broadcasted_iota(jnp.int32, (2 * kb, kb), 0), kb))
        e, tot = _sb_parts(z, tri, causal)
        att = jnp.where(causal, jnp.exp2(e), 0.0)
        pv = _dot(att, v_ref[ks, ls])
        if r0 == 0:
            acc[p] = pv
            return tot
        zero = jnp.zeros((r0, LANES), F32)
        acc[p] = jnp.concatenate([zero, pv[:kb], zero, pv[kb:]], axis=0)
        t0 = jnp.broadcast_to(tot, (2 * kb, LANES))
        return jnp.concatenate([zero, t0[:kb], zero, t0[kb:]], axis=0)[:, 0:1]

    carries = [first_block(p) for p in range(npair)]
    for sub in range(nsub - 2, -1, -1):
        for p in range(npair):
            carries[p] = block(p, i * nsub + sub, carries[p], sub)

    def step(s, carry):
        carry = list(carry)
        for u in range(nsub):
            for p in range(npair):
                carry[p] = block(p, (i - s) * nsub - 1 - u, carry[p], None)
        return tuple(carry)

    lax.fori_loop(0, i, step, tuple(carries))
    for p in range(npair):
        o_ref[:, p * LANES:(p + 1) * LANES] = jnp.where(
            lane < dh, acc[p, 0:tq, :], acc[p, tq:2 * tq, :]).astype(o_ref.dtype)


def _attn_s_one(ppb, page, r, n_heads, dh, bias_ref, q, kn_rows, vn_rows, tri, k_refs, v_refs):
    n_pages = len(k_refs)
    kb = ppb * page
    hd = n_heads * dh
    hq = n_heads * r
    row_h = _idiv(lax.broadcasted_iota(jnp.int32, (hq, hd), 0), r)
    lane_h = _idiv(lax.broadcasted_iota(jnp.int32, (hq, hd), 1), dh)
    own = row_h == lane_h
    qrep = jnp.broadcast_to(q[None], (n_heads, r, hd)).reshape(hq, hd)
    qbd = jnp.where(own, qrep, 0.0).astype(BF16)
    rh = _idiv(lax.broadcasted_iota(jnp.int32, (hq, kb), 0), r)
    bias = jnp.zeros((hq, kb), F32)
    for h in range(n_heads):
        bias = jnp.where(rh == h, bias_ref[h] * LOG2E, bias)

    qi = _imod(lax.broadcasted_iota(jnp.int32, (hq, LANES), 0), r)
    kj = lax.broadcasted_iota(jnp.int32, (hq, LANES), 1)
    causal = kj < qi
    pad = jnp.zeros((LANES - r, hd), F32)
    kn = jnp.concatenate([kn_rows, pad], axis=0).astype(BF16)
    vn = jnp.concatenate([vn_rows, pad], axis=0).astype(BF16)
    z = _dot_nt(qbd, kn) + bias[:, :LANES]
    e, carry = _sb_parts(z, tri[:LANES, :LANES], causal)
    att_new = jnp.where(causal, jnp.exp2(e), 0.0)
    nb = n_pages // ppb

    def block_t(refs, j):
        pages = [refs[j * ppb + u][...] for u in range(ppb)]
        return jnp.concatenate(pages, axis=1).astype(BF16)

    zs = [_dot(qbd, block_t(k_refs, j)) + bias for j in range(nb)]
    ss = [_softplus2(zj) for zj in zs]
    cs = [_dot(sj, tri) for sj in ss]
    acc = _dot(att_new, vn)
    for j in range(nb - 1, -1, -1):
        att = jnp.exp2(zs[j] - ss[j] - cs[j] - carry)
        carry = carry + cs[j][:, 0:1] + ss[j][:, 0:1]
        acc = acc + _dot_nt(att, block_t(v_refs, j))
    o = jnp.where(own, acc, 0.0).reshape(n_heads, r, hd)
    return jnp.sum(o, axis=0)


def _attn_body(cfg, pt_ref, bias_ref, q_ref, k_ref, v_ref, tri_ref, qs_ref, kn_ref, vn_ref, *refs):
    del pt_ref
    tq, kb, dh, npair, n_heads, r, spp, n_pages, ppb, page = cfg
    npg = spp * n_pages
    k_refs, v_refs = refs[:npg], refs[npg:2 * npg]
    o_ref, os_ref, acc = refs[2 * npg:]
    _attn_p_body(tq, kb, dh, npair, bias_ref, q_ref, k_ref, v_ref, tri_ref, o_ref, acc)
    tri = tri_ref[...]
    for u in range(spp):
        pages = slice(u * n_pages, (u + 1) * n_pages)
        os_ref[u] = _attn_s_one(ppb, page, r, n_heads, dh, bias_ref, qs_ref[u], kn_ref[u], vn_ref[u],
                                tri, k_refs[pages], v_refs[pages])


def _attention(q_p, kb_p, vb_p, q_s, k_new, v_new, cache_kt, cache_vt, page_table, bias, n_heads):
    b, t, hd = q_p.shape
    ns, r, _ = q_s.shape
    dh = hd // n_heads
    assert 2 * dh == LANES and n_heads % 2 == 0
    npair = 2 if n_heads % 4 == 0 else 1
    kb = min(256, t)
    tq = min(2 * kb, t)
    assert t % tq == 0 and tq % kb == 0
    width = npair * LANES
    n_grp, nq = n_heads // (2 * npair), t // tq
    steps = b * n_grp * nq
    page = cache_kt.shape[2]
    n_pages = page_table.shape[1]
    ppb = max(1, kb // page)
    assert ppb * page == kb and n_pages % ppb == 0 and page % LANES == 0 and r <= LANES
    assert ns % steps == 0, "sample sequences are spread evenly over the prompt grid steps"
    spp = ns // steps
    tri = (jnp.arange(kb)[:, None] > jnp.arange(kb)[None, :]).astype(BF16)

    def step(bi, g, i):
        return (bi * n_grp + g) * nq + i

    seq = pl.BlockSpec((spp, r, hd), lambda bi, g, i, pt: (step(bi, g, i), 0, 0))

    def page_spec(u, p):
        return pl.BlockSpec((None, hd, page),
                            lambda bi, g, i, pt: (pt[(step(bi, g, i) * spp + u) * n_pages + p], 0, 0))

    pages = [page_spec(u, p) for u in range(spp) for p in range(n_pages)]
    tile = pl.BlockSpec((None, tq, width), lambda bi, g, i, pt: (bi, i, g))
    whole = pl.BlockSpec((None, t, width), lambda bi, g, i, pt: (bi, 0, g))
    grid_spec = pltpu.PrefetchScalarGridSpec(
        num_scalar_prefetch=1,
        grid=(b, n_grp, nq),
        in_specs=[pl.BlockSpec(memory_space=pltpu.SMEM), tile, whole, whole,
                  pl.BlockSpec((kb, kb), lambda bi, g, i, pt: (0, 0)), seq, seq, seq] + pages * 2,
        out_specs=[tile, seq],
        scratch_shapes=[pltpu.VMEM((npair, 2 * tq, LANES), F32)],
    )
    cfg = (tq, kb, dh, npair, n_heads, r, spp, n_pages, ppb, page)
    return pl.pallas_call(
        functools.partial(_attn_body, cfg),
        grid_spec=grid_spec,
        out_shape=[jax.ShapeDtypeStruct((b, t, hd), BF16), jax.ShapeDtypeStruct((ns, r, hd), F32)],
        compiler_params=_cparams("arbitrary", "arbitrary", "arbitrary"),
        name="attn",
    )(page_table.reshape(-1), bias, q_p, kb_p, vb_p, tri, q_s, k_new, v_new,
      *([cache_kt] * (spp * n_pages)), *([cache_vt] * (spp * n_pages)))


def kernel(x_prompt, x_sample, c_prompt, c_sample, state_conv_a, state_ffn_conv, cache_k, cache_v, page_table, w_ada, b_ada, g_pre_mix, g_post_mix, g_pre_ffn, g_post_ffn, w_a_in, w_a_dw, b_a_dw, ln_a_g, ln_a_b, w_a_out, g_kv, w_kv, w_q, w_o, b_sb, w_ffn_up, w_ffn_dw, b_ffn_dw, w_ffn_down):
    depth, d = g_pre_mix.shape
    n_a = w_a_in.shape[0]
    n_heads, dh = cache_k.shape[2], cache_k.shape[3]
    hd = n_heads * dh
    dff = w_ffn_down.shape[1]

    def vec(a):
        return a.reshape(a.shape[0], 1, a.shape[1])

    g_pre_mix, g_post_mix, g_pre_ffn, g_post_ffn = map(vec, (g_pre_mix, g_post_mix, g_pre_ffn, g_post_ffn))
    b_a_dw, ln_a_g, ln_a_b, b_ffn_dw = map(vec, (b_a_dw, ln_a_g, ln_a_b, b_ffn_dw))
    w_a_in, w_a_out, w_kv, w_q, w_o, w_ffn_up, w_ffn_down = (
        w.astype(BF16) for w in (w_a_in, w_a_out, w_kv, w_q, w_o, w_ffn_up, w_ffn_down))
    g_kv = g_kv.reshape(1, d)

    mod_p, mod_s = _ada(c_prompt, c_sample, w_ada, b_ada)
    n_phys, page = cache_k.shape[:2]
    past_kv = tuple(c.transpose(0, 2, 3, 1).reshape(n_phys, hd, page) for c in (cache_k, cache_v))
    groups = [
        dict(x=x_prompt, mod=mod_p.reshape(depth, -1, 6, d), conv_state=None, ffn_state=None,
             lay_a=_Layout(x_prompt.shape, True, 512, 0), lay_f=_Layout(x_prompt.shape, True, 512, 0),
             lay_d=_Layout(x_prompt.shape, True, 512, 0)),
        dict(x=x_sample, mod=mod_s.reshape(depth, -1, 6, d), conv_state=state_conv_a,
             ffn_state=state_ffn_conv,
             lay_a=_Layout(x_sample.shape, False, 0, 32), lay_f=_Layout(x_sample.shape, False, 0, 32),
             lay_d=_Layout(x_sample.shape, False, 0, 64)),
    ]
    for grp in groups:
        grp["conv_tails"], grp["ffn_tails"] = [], []
    for l in range(depth):
        if l < n_a:
            for grp in groups:
                grp["x"], tail = _mixer_a(grp["lay_a"], l, grp["x"], grp["mod"], grp["conv_state"],
                                          g_pre_mix, g_post_mix, w_a_in, w_a_dw, b_a_dw, ln_a_g, ln_a_b,
                                          w_a_out)
                grp["conv_tails"].append(tail)
        else:
            j = l - n_a
            q_p, q_s = (_q_proj(grp["lay_d"], l, j, grp["x"], grp["mod"], g_pre_mix, w_q,
                                dh ** -0.5 * LOG2E) for grp in groups)
            (_, _, kb_p, vb_p), (k_new, v_new) = groups[0]["kv"], groups[1]["kv"]
            outs = _attention(q_p, kb_p, vb_p, q_s, k_new, v_new, *past_kv, page_table, b_sb[j], n_heads)
            for grp, o in zip(groups, outs):
                grp["x"] = _o_proj(grp["lay_d"], l, j, o, grp["x"], grp["mod"], g_post_mix, w_o)
        for grp in groups:
            grp["x"], tail = _ffn(grp["lay_f"], l, grp["x"], grp["mod"], grp["ffn_state"], g_pre_ffn,
                                  g_post_ffn, w_ffn_up, w_ffn_dw, b_ffn_dw, w_ffn_down)
            grp["ffn_tails"].append(tail)
            if l == n_a - 1:
                grp["kv"] = _kv_proj(grp["lay_d"], grp["x"], g_kv, w_kv)
    results = []
    for grp in groups:
        x = grp["x"]
        shp = x.shape[:2] + (n_heads, dh)
        results.append((x, jnp.stack(grp["conv_tails"]), jnp.stack(grp["ffn_tails"]),
                        grp["kv"][0].reshape(shp), grp["kv"][1].reshape(shp)))
    (yp, cap, ffp, kp, vp), (ys, cas, ffs, ks, vs) = results
    return (yp, ys, cap, cas, ffp, ffs, kp, vp, ks, vs)
```
